```python
import math
import jax, jax.numpy as jnp
from jax import lax
import numpy as np

D_MODEL = 1024
BATCH = 16
SEQ = 2048
DEPTH = 1

HEAD_DIM = 64
A_Q_HEADS = 8
A_KV_HEADS = 2
B_HEADS = 8
GRID_W = 64
ROPE_THETA = 10000.0
Q_BLOCK = 128
DIL_BRANCHES = ((128, 1), (512, 4), (2048, 16))
DIL_BLOCK = max(w // (2 * r) for w, r in DIL_BRANCHES)
N_BUCKETS = 32
BUCKET_MAX_DIST = 1024
D_FF = 4 * D_MODEL
EPS = 1e-6
NEG = -1e30
A_WIDTH = A_Q_HEADS * HEAD_DIM
A_KV_WIDTH = A_KV_HEADS * HEAD_DIM
B_WIDTH = B_HEADS * HEAD_DIM
IN_WIDTH = A_WIDTH + 2 * A_KV_WIDTH + 3 * B_WIDTH
MIX_WIDTH = A_WIDTH + B_WIDTH

kernel_name = "hybrid_axial_gqa_dilated_attn_block"


def rms_norm(x, g):
    xf = x.astype(jnp.float32)
    y = xf * lax.rsqrt(jnp.mean(xf * xf, axis=-1, keepdims=True) + EPS)
    return (y * g.astype(jnp.float32)).astype(x.dtype)


def axial_rope_tables(S):
    ROWS = S // GRID_W
    row = jnp.repeat(jnp.arange(ROWS, dtype=jnp.float32), GRID_W)
    col = jnp.tile(jnp.arange(GRID_W, dtype=jnp.float32), ROWS)
    n_freq = HEAD_DIM // 4
    inv = ROPE_THETA ** (-jnp.arange(n_freq, dtype=jnp.float32) / n_freq)
    ang_r = row[:, None] * inv[None, :]
    ang_c = col[:, None] * inv[None, :]
    return jnp.cos(ang_r), jnp.sin(ang_r), jnp.cos(ang_c), jnp.sin(ang_c)


def rotate_half(x, cos, sin):
    x1, x2 = jnp.split(x, 2, axis=-1)
    return jnp.concatenate([x1 * cos - x2 * sin, x2 * cos + x1 * sin], axis=-1)


def axial_rope(x, tabs):
    cos_r, sin_r, cos_c, sin_c = tabs
    xf = x.astype(jnp.float32)
    a = HEAD_DIM // 2
    y = jnp.concatenate([rotate_half(xf[..., :a], cos_r, sin_r),
                         rotate_half(xf[..., a:], cos_c, sin_c)], axis=-1)
    return y.astype(x.dtype)


def gqa_axial_attention(q, k, v, gq, gk):
    B, S = q.shape[0], q.shape[1]
    tabs = axial_rope_tables(S)
    q = axial_rope(rms_norm(q, gq).transpose(0, 2, 1, 3), tabs)
    k = axial_rope(rms_norm(k, gk).transpose(0, 2, 1, 3), tabs)
    v = v.transpose(0, 2, 1, 3)
    G = A_Q_HEADS // A_KV_HEADS
    nb = S // Q_BLOCK
    q = q * (HEAD_DIM ** -0.5)
    qb = q.reshape(B, A_KV_HEADS, G, nb, Q_BLOCK, HEAD_DIM).transpose(3, 0, 1, 2, 4, 5)

    def block(qi):
        s = jnp.einsum('bkgqd,bksd->bkgqs', qi, k).astype(jnp.float32)
        p = jax.nn.softmax(s, axis=-1).astype(v.dtype)
        return jnp.einsum('bkgqs,bksd->bkgqd', p, v)

    o = lax.map(block, qb)
    return o.transpose(1, 0, 4, 2, 3, 5).reshape(B, S, A_WIDTH)


def t5_bucket(rel):
    nb_half = N_BUCKETS // 2
    max_exact = nb_half // 2
    ret = jnp.where(rel > 0, nb_half, 0)
    n = jnp.abs(rel)
    large = max_exact + (jnp.log(jnp.maximum(n, 1).astype(jnp.float32) / max_exact)
                         / math.log(BUCKET_MAX_DIST / max_exact)
                         * (nb_half - max_exact)).astype(jnp.int32)
    large = jnp.minimum(large, nb_half - 1)
    return ret + jnp.where(n < max_exact, n, large)


def band_blocks(xp, nb, qb):
    Bq, H, r, _, D = xp.shape
    parts = [xp[:, :, :, j * qb:j * qb + nb * qb].reshape(Bq, H, r, nb, qb, D) for j in range(3)]
    return jnp.concatenate(parts, axis=4)


def dilated_branch(q, k, v, rel_bias, window, dilation):
    B, H, S, D = q.shape
    r = dilation
    L = S // r
    half = window // (2 * r)
    Qb = DIL_BLOCK
    nb = -(-L // Qb)
    Lp = nb * Qb

    def to_phase(t):
        return t.reshape(B, H, L, r, D).transpose(0, 1, 3, 2, 4)

    qp = jnp.pad(to_phase(q), ((0, 0), (0, 0), (0, 0), (0, Lp - L), (0, 0))).reshape(B, H, r, nb, Qb, D)
    pad_kv = ((0, 0), (0, 0), (0, 0), (Qb, Lp - L + Qb), (0, 0))
    kb = band_blocks(jnp.pad(to_phase(k), pad_kv), nb, Qb)
    vb = band_blocks(jnp.pad(to_phase(v), pad_kv), nb, Qb)

    kk = jnp.arange(3 * Qb)
    qq = jnp.arange(Qb)
    rel = kk[None, :] - Qb - qq[:, None]
    kpos = jnp.arange(nb)[:, None] * Qb + kk[None, :] - Qb
    mask = (jnp.abs(rel) <= half)[None] & ((kpos >= 0) & (kpos < L))[:, None, :]
    bias = rel_bias[t5_bucket(rel * r)].transpose(2, 0, 1).astype(jnp.float32)

    s = jnp.einsum('bhrnqd,bhrnkd->bhrnqk', qp, kb).astype(jnp.float32) + bias[:, None, None]
    s = jnp.where(mask, s, NEG)
    m = jnp.max(s, axis=-1, keepdims=True)
    e = jnp.exp(s - m)
    den = jnp.sum(e, axis=-1, keepdims=True)
    o = jnp.einsum('bhrnqk,bhrnkd->bhrnqd', e, vb.astype(jnp.float32)) / den
    lse = (m + jnp.log(den))[..., 0]
    o = o.reshape(B, H, r, Lp, D)[:, :, :, :L].transpose(0, 1, 3, 2, 4).reshape(B, H, S, D)
    lse = lse.reshape(B, H, r, Lp)[:, :, :, :L].transpose(0, 1, 3, 2).reshape(B, H, S)
    return o, lse


def dilated_mixture_attention(q, k, v, rel_bias):
    B, S = q.shape[0], q.shape[1]
    q = q.transpose(0, 2, 1, 3) * (HEAD_DIM ** -0.5)
    k = k.transpose(0, 2, 1, 3)
    v = v.transpose(0, 2, 1, 3)
    outs, lses = [], []
    for window, dilation in DIL_BRANCHES:
        o, l = dilated_branch(q, k, v, rel_bias, window, dilation)
        outs.append(o)
        lses.append(l)
    alpha = jax.nn.softmax(jnp.stack(lses, axis=0), axis=0)
    y = jnp.einsum('nbhs,nbhsd->bhsd', alpha, jnp.stack(outs, axis=0))
    return y.transpose(0, 2, 1, 3).reshape(B, S, B_WIDTH).astype(v.dtype)


def setup_inputs(seed: int = 0) -> dict:
    key = jax.random.key(seed)
    ks = jax.random.split(key, 16)
    f32 = jnp.float32

    def gain(k, shape):
        return jnp.ones(shape, f32) + 0.05 * jax.random.normal(k, shape, f32)

    return {
        "x": jax.random.normal(ks[0], (BATCH, SEQ, D_MODEL), f32),
        "attn_norm_g": gain(ks[1], (DEPTH, D_MODEL)),
        "w_in": jax.random.normal(ks[2], (DEPTH, D_MODEL, IN_WIDTH), f32) * D_MODEL ** -0.5,
        "q_norm_g": gain(ks[3], (DEPTH, HEAD_DIM)),
        "k_norm_g": gain(ks[4], (DEPTH, HEAD_DIM)),
        "rel_bias": 0.5 * jax.random.normal(ks[5], (N_BUCKETS, B_HEADS), f32),
        "out_norm_a_g": gain(ks[6], (DEPTH, A_WIDTH)),
        "out_norm_b_g": gain(ks[7], (DEPTH, B_WIDTH)),
        "w_out": jax.random.normal(ks[8], (DEPTH, MIX_WIDTH, D_MODEL), f32) * MIX_WIDTH ** -0.5,
        "mlp_norm_g": gain(ks[9], (DEPTH, D_MODEL)),
        "w_up": jax.random.normal(ks[10], (DEPTH, D_MODEL, D_FF), f32) * D_MODEL ** -0.5,
        "w_down": jax.random.normal(ks[11], (DEPTH, D_FF, D_MODEL), f32) * D_FF ** -0.5,
        "final_norm_g": gain(ks[12], (D_MODEL,)),
    }


def reference(x, attn_norm_g, w_in, q_norm_g, k_norm_g, rel_bias, out_norm_a_g, out_norm_b_g,
              w_out, mlp_norm_g, w_up, w_down, final_norm_g):
    B, S, _ = x.shape
    h = x
    o1 = A_WIDTH
    o2 = o1 + A_KV_WIDTH
    o3 = o2 + A_KV_WIDTH
    o4 = o3 + B_WIDTH
    o5 = o4 + B_WIDTH
    for l in range(DEPTH):
        n = rms_norm(h, attn_norm_g[l])
        proj = jnp.einsum('bsd,de->bse', n, w_in[l])
        qa = proj[..., :o1].reshape(B, S, A_Q_HEADS, HEAD_DIM)
        ka = proj[..., o1:o2].reshape(B, S, A_KV_HEADS, HEAD_DIM)
        va = proj[..., o2:o3].reshape(B, S, A_KV_HEADS, HEAD_DIM)
        qb = proj[..., o3:o4].reshape(B, S, B_HEADS, HEAD_DIM)
        kb = proj[..., o4:o5].reshape(B, S, B_HEADS, HEAD_DIM)
        vb = proj[..., o5:].reshape(B, S, B_HEADS, HEAD_DIM)
        ya = gqa_axial_attention(qa, ka, va, q_norm_g[l], k_norm_g[l])
        yb = dilated_mixture_attention(qb, kb, vb, rel_bias)
        mix = jnp.concatenate([rms_norm(ya, out_norm_a_g[l]), rms_norm(yb, out_norm_b_g[l])], axis=-1)
        h = h + jnp.einsum('bse,ed->bsd', mix, w_out[l])
        u = jnp.einsum('bsd,df->bsf', rms_norm(h, mlp_norm_g[l]), w_up[l])
        h = h + jnp.einsum('bsf,fd->bsd', jnp.square(jax.nn.relu(u)), w_down[l])
    return rms_norm(h, final_norm_g)
```

```python
import functools
import math

import jax
import jax.numpy as jnp
import numpy as np
from jax import lax
from jax.experimental import pallas as pl
from jax.experimental.pallas import tpu as pltpu

HEAD_DIM = 64
A_Q_HEADS = 8
A_KV_HEADS = 2
B_HEADS = 8
GRID_W = 64
ROPE_THETA = 10000.0
DIL_BRANCHES = ((128, 1), (512, 4), (2048, 16))
N_BUCKETS = 32
BUCKET_MAX_DIST = 1024
EPS = 1e-6
NEG = -1e30

A_WIDTH = A_Q_HEADS * HEAD_DIM
A_KV_WIDTH = A_KV_HEADS * HEAD_DIM
B_WIDTH = B_HEADS * HEAD_DIM
Q_SCALE = HEAD_DIM ** -0.5

LANES = 128
TM_IN = 512
TQ = 128
TM_OUT = 512
FF_CHUNK = 1024
VMEM_LIMIT = 56 * 1024 * 1024

F32 = jnp.float32
BF16 = jnp.bfloat16


def _rms(x, g):
    return x * lax.rsqrt(jnp.mean(x * x, axis=-1, keepdims=True) + EPS) * g


def _inproj_kernel(x_ref, g_ref, w_ref, gq_ref, gk_ref, cos_ref, sin_ref, ones_ref,
                   qa_ref, kat_ref, va_ref, qb_ref, kbt_ref, vb_ref):
    n = _rms(x_ref[...], g_ref[...]).astype(BF16)
    cos = cos_ref[...]
    sin = sin_ref[...]
    ones2 = ones_ref[...]
    lane = lax.broadcasted_iota(jnp.int32, (TM_IN, LANES), 1)
    first_half = (lane % 32) < 16

    def head_norm_rope(blk, gain):
        sq = blk * blk
        hi = sq.astype(BF16)
        lo = (sq - hi.astype(F32)).astype(BF16)
        ss = jnp.dot(jnp.concatenate([hi, lo], axis=1), ones2, preferred_element_type=F32)
        xn = blk * lax.rsqrt(ss * (1.0 / HEAD_DIM) + EPS) * gain
        partner = jnp.where(first_half, pltpu.roll(xn, LANES - 16, 1), pltpu.roll(xn, 16, 1))
        return xn * cos + partner * sin

    o_ka = A_WIDTH
    o_va = o_ka + A_KV_WIDTH
    o_qb = o_va + A_KV_WIDTH
    o_kb = o_qb + B_WIDTH
    o_vb = o_kb + B_WIDTH

    pq = jnp.dot(n, w_ref[:, 0:o_ka], preferred_element_type=F32)
    gq = gq_ref[...]
    for j in range(A_WIDTH // LANES):
        sl = slice(j * LANES, (j + 1) * LANES)
        qa_ref[:, sl] = (head_norm_rope(pq[:, sl], gq) * Q_SCALE).astype(BF16)

    pk = jnp.dot(n, w_ref[:, o_ka:o_va], preferred_element_type=F32)
    kt = head_norm_rope(pk, gk_ref[...]).T.astype(BF16)
    k0 = kt[0:HEAD_DIM]
    k1 = kt[HEAD_DIM:2 * HEAD_DIM]
    kat_ref[...] = jnp.concatenate([k0, k0, k1, k1], axis=0)

    pv = jnp.dot(n, w_ref[:, o_va:o_qb], preferred_element_type=F32).astype(BF16)
    v0 = pv[:, 0:HEAD_DIM]
    v1 = pv[:, HEAD_DIM:2 * HEAD_DIM]
    va_ref[...] = jnp.concatenate([v0, v0, v1, v1], axis=1)

    qb_ref[...] = (jnp.dot(n, w_ref[:, o_qb:o_kb], preferred_element_type=F32) * Q_SCALE).astype(BF16)
    pkb = jnp.dot(n, w_ref[:, o_kb:o_vb], preferred_element_type=F32)
    for j in range(B_WIDTH // LANES):
        sl = slice(j * LANES, (j + 1) * LANES)
        kbt_ref[sl, :] = pkb[:, sl].T.astype(BF16)
    vb_ref[...] = jnp.dot(n, w_ref[:, o_vb:], preferred_element_type=F32).astype(BF16)


def _rope_tables(S):
    rows = S // GRID_W
    row = jnp.repeat(jnp.arange(rows, dtype=F32), GRID_W)
    col = jnp.tile(jnp.arange(GRID_W, dtype=F32), rows)
    n_freq = HEAD_DIM // 4
    inv = ROPE_THETA ** (-jnp.arange(n_freq, dtype=F32) / n_freq)
    ang_r = row[:, None] * inv[None, :]
    ang_c = col[:, None] * inv[None, :]
    cos_h = jnp.concatenate([jnp.cos(ang_r)] * 2 + [jnp.cos(ang_c)] * 2, axis=1)
    sin_h = jnp.concatenate([-jnp.sin(ang_r), jnp.sin(ang_r), -jnp.sin(ang_c), jnp.sin(ang_c)], axis=1)
    return jnp.tile(cos_h, (1, 2)), jnp.tile(sin_h, (1, 2))


def _in_projection(x, g, w_bf16, gq, gk):
    B, S, D = x.shape
    in_width = w_bf16.shape[1]
    cos, sin = _rope_tables(S)
    r = np.arange(2 * LANES)[:, None]
    c = np.arange(LANES)[None, :]
    ones2 = jnp.asarray(((r % LANES) // HEAD_DIM) == (c // HEAD_DIM), dtype=BF16)
    nt = S // TM_IN
    tok = lambda t, b: (b, t, 0)
    const = lambda t, b: (0, 0)
    out_shape = (
        jax.ShapeDtypeStruct((B, S, A_WIDTH), BF16),
        jax.ShapeDtypeStruct((B, 2 * A_KV_WIDTH, S), BF16),
        jax.ShapeDtypeStruct((B, S, 2 * A_KV_WIDTH), BF16),
        jax.ShapeDtypeStruct((B, S, B_WIDTH), BF16),
        jax.ShapeDtypeStruct((B, B_WIDTH, S), BF16),
        jax.ShapeDtypeStruct((B, S, B_WIDTH), BF16),
    )
    return pl.pallas_call(
        _inproj_kernel,
        grid=(nt, B),
        in_specs=[
            pl.BlockSpec((None, TM_IN, D), tok),
            pl.BlockSpec((1, D), const),
            pl.BlockSpec((D, in_width), const),
            pl.BlockSpec((1, LANES), const),
            pl.BlockSpec((1, LANES), const),
            pl.BlockSpec((TM_IN, LANES), lambda t, b: (t, 0)),
            pl.BlockSpec((TM_IN, LANES), lambda t, b: (t, 0)),
            pl.BlockSpec((2 * LANES, LANES), const),
        ],
        out_specs=(
            pl.BlockSpec((None, TM_IN, A_WIDTH), tok),
            pl.BlockSpec((None, 2 * A_KV_WIDTH, TM_IN), lambda t, b: (b, 0, t)),
            pl.BlockSpec((None, TM_IN, 2 * A_KV_WIDTH), tok),
            pl.BlockSpec((None, TM_IN, B_WIDTH), tok),
            pl.BlockSpec((None, B_WIDTH, TM_IN), lambda t, b: (b, 0, t)),
            pl.BlockSpec((None, TM_IN, B_WIDTH), tok),
        ),
        out_shape=out_shape,
        compiler_params=pltpu.CompilerParams(
            dimension_semantics=("arbitrary", "arbitrary"), vmem_limit_bytes=VMEM_LIMIT),
        name="in_projection",
    )(x, g, w_bf16, gq, gk, cos, sin, ones2)


def _split_pair(q2):
    lane = lax.broadcasted_iota(jnp.int32, q2.shape, 1)
    low = lane < HEAD_DIM
    zero = jnp.zeros_like(q2)
    return jnp.concatenate([jnp.where(low, q2, zero), jnp.where(low, zero, q2)], axis=0)


def _softmax_pv(s, v2):
    m = jnp.max(s, axis=-1, keepdims=True)
    p = jnp.exp(s - m)
    l = jnp.sum(p, axis=-1, keepdims=True)
    o = jnp.dot(p.astype(BF16), v2, preferred_element_type=F32)
    return o * (1.0 / l)


def _merge_pair(o2):
    t = o2.shape[0] // 2
    lane = lax.broadcasted_iota(jnp.int32, (t, LANES), 1)
    return jnp.where(lane < HEAD_DIM, o2[0:t], o2[t:])


def _attn_a_kernel(q_ref, kt_ref, v_ref, g_ref, o_ref):
    group = A_Q_HEADS // A_KV_HEADS
    blocks = []
    for kv in range(A_KV_HEADS):
        kt = kt_ref[kv * LANES:(kv + 1) * LANES, :]
        v2 = v_ref[:, kv * LANES:(kv + 1) * LANES]
        base = kv * group * HEAD_DIM
        qs = jnp.concatenate(
            [_split_pair(q_ref[:, base + j * LANES: base + (j + 1) * LANES])
             for j in range(group // 2)], axis=0)
        s = jnp.dot(qs, kt, preferred_element_type=F32)
        o = _softmax_pv(s, v2)
        for j in range(group // 2):
            blocks.append(_merge_pair(o[2 * j * TQ:(2 * j + 2) * TQ]))
    y = jnp.concatenate(blocks, axis=1)
    o_ref[...] = _rms(y, g_ref[...]).astype(BF16)


def _attention_a(qa, kat, va, g):
    B, S, _ = qa.shape
    return pl.pallas_call(
        _attn_a_kernel,
        grid=(B, S // TQ),
        in_specs=[
            pl.BlockSpec((None, TQ, A_WIDTH), lambda b, i: (b, i, 0)),
            pl.BlockSpec((None, 2 * A_KV_WIDTH, S), lambda b, i: (b, 0, 0)),
            pl.BlockSpec((None, S, 2 * A_KV_WIDTH), lambda b, i: (b, 0, 0)),
            pl.BlockSpec((1, A_WIDTH), lambda b, i: (0, 0)),
        ],
        out_specs=pl.BlockSpec((None, TQ, A_WIDTH), lambda b, i: (b, i, 0)),
        out_shape=jax.ShapeDtypeStruct((B, S, A_WIDTH), BF16),
        compiler_params=pltpu.CompilerParams(
            dimension_semantics=("arbitrary", "arbitrary"), vmem_limit_bytes=VMEM_LIMIT),
        name="attention_a",
    )(qa, kat, va, g)


def _t5_bucket(rel):
    nb_half = N_BUCKETS // 2
    max_exact = nb_half // 2
    ret = jnp.where(rel > 0, nb_half, 0)
    n = jnp.abs(rel)
    large = max_exact + (jnp.log(jnp.maximum(n, 1).astype(F32) / max_exact)
                         / math.log(BUCKET_MAX_DIST / max_exact)
                         * (nb_half - max_exact)).astype(jnp.int32)
    large = jnp.minimum(large, nb_half - 1)
    return ret + jnp.where(n < max_exact, n, large)


def _bias_kernel(rbt_ref, code_ref, logc_ref, o_ref):
    code = code_ref[...]
    rbt = rbt_ref[...]
    acc = jnp.zeros(o_ref.shape, F32)
    for b in range(N_BUCKETS):
        acc = jnp.where(code == b, rbt[:, b:b + 1], acc)
    o_ref[...] = acc + logc_ref[...]


def _offset_bias(rel_bias, S):
    d = jnp.arange(2 * S, dtype=jnp.int32) - S
    mult = np.zeros((2 * S,), np.float64)
    dn = np.arange(2 * S) - S
    for window, r in DIL_BRANCHES:
        mult += (dn % r == 0) & (np.abs(dn) <= (window // (2 * r)) * r)
    logc = np.where(mult > 0, np.log(np.maximum(mult, 1.0)), NEG).astype(np.float32)
    code = _t5_bucket(d).reshape(1, 2 * S)
    return pl.pallas_call(
        _bias_kernel,
        out_shape=jax.ShapeDtypeStruct((B_HEADS, 2 * S), F32),
        name="offset_bias",
    )(rel_bias.T, code, jnp.asarray(logc).reshape(1, 2 * S))


def _toeplitz_chunks(g1, S):
    H, W = g1.shape
    v = jnp.concatenate([g1, jnp.full((H, 1), NEG, F32)], axis=1)
    rows = jnp.tile(v, (1, TQ))[:, :TQ * W].reshape(H, TQ, W)
    return rows.reshape(H, TQ, W // LANES, LANES).transpose(0, 2, 1, 3)


def _attn_b_kernel(q_ref, kt_ref, v_ref, t_ref, g_ref, o_ref, *, n_key_chunks, n_q_blocks):
    i = pl.program_id(1)
    base = n_q_blocks - i
    blocks = []
    for j in range(B_HEADS // 2):
        sl = slice(j * LANES, (j + 1) * LANES)
        qs = _split_pair(q_ref[:, sl])
        s = jnp.dot(qs, kt_ref[sl, :], preferred_element_type=F32)
        bias = jnp.concatenate(
            [jnp.concatenate([t_ref[2 * j + h, base + c] for c in range(n_key_chunks)], axis=1)
             for h in range(2)], axis=0)
        blocks.append(_merge_pair(_softmax_pv(s + bias, v_ref[:, sl])))
    y = jnp.concatenate(blocks, axis=1)
    o_ref[...] = _rms(y, g_ref[...]).astype(BF16)


def _attention_b(qb, kbt, vb, table, g):
    B, S, _ = qb.shape
    nq = S // TQ
    kern = functools.partial(_attn_b_kernel, n_key_chunks=S // LANES, n_q_blocks=nq)
    return pl.pallas_call(
        kern,
        grid=(B, nq),
        in_specs=[
            pl.BlockSpec((None, TQ, B_WIDTH), lambda b, i: (b, i, 0)),
            pl.BlockSpec((None, B_WIDTH, S), lambda b, i: (b, 0, 0)),
            pl.BlockSpec((None, S, B_WIDTH), lambda b, i: (b, 0, 0)),
            pl.BlockSpec(table.shape, lambda b, i: (0, 0, 0, 0), pipeline_mode=pl.Buffered(1)),
            pl.BlockSpec((1, B_WIDTH), lambda b, i: (0, 0)),
        ],
        out_specs=pl.BlockSpec((None, TQ, B_WIDTH), lambda b, i: (b, i, 0)),
        out_shape=jax.ShapeDtypeStruct((B, S, B_WIDTH), BF16),
        compiler_params=pltpu.CompilerParams(
            dimension_semantics=("arbitrary", "arbitrary"), vmem_limit_bytes=VMEM_LIMIT),
        name="attention_b",
    )(qb, kbt, vb, table, g)


def _out_mlp_kernel(x_ref, ma_ref, mb_ref, wo_ref, gm_ref, wup_ref, wdn_ref, gf_ref, o_ref):
    mix = jnp.concatenate([ma_ref[...], mb_ref[...]], axis=1)
    h = x_ref[...] + jnp.dot(mix, wo_ref[...], preferred_element_type=F32)
    hn = _rms(h, gm_ref[...]).astype(BF16)
    acc = h
    d_ff = wup_ref.shape[1]
    for c in range(d_ff // FF_CHUNK):
        sl = slice(c * FF_CHUNK, (c + 1) * FF_CHUNK)
        u = jnp.maximum(jnp.dot(hn, wup_ref[:, sl], preferred_element_type=F32), 0.0)
        acc = acc + jnp.dot((u * u).astype(BF16), wdn_ref[sl, :], preferred_element_type=F32)
    o_ref[...] = _rms(acc, gf_ref[...])


def _out_mlp(x2, ma, mb, wo, gm, wup, wdn, gf):
    N, D = x2.shape
    d_ff = wup.shape[1]
    tok = lambda t: (t, 0)
    const = lambda t: (0, 0)
    resident = functools.partial(pl.BlockSpec, index_map=const, pipeline_mode=pl.Buffered(1))
    return pl.pallas_call(
        _out_mlp_kernel,
        grid=(N // TM_OUT,),
        in_specs=[
            pl.BlockSpec((TM_OUT, D), tok),
            pl.BlockSpec((TM_OUT, A_WIDTH), tok),
            pl.BlockSpec((TM_OUT, B_WIDTH), tok),
            resident((A_WIDTH + B_WIDTH, D)),
            pl.BlockSpec((1, D), const),
            resident((D, d_ff)),
            resident((d_ff, D)),
            pl.BlockSpec((1, D), const),
        ],
        out_specs=pl.BlockSpec((TM_OUT, D), tok),
        out_shape=jax.ShapeDtypeStruct((N, D), F32),
        compiler_params=pltpu.CompilerParams(
            dimension_semantics=("arbitrary",), vmem_limit_bytes=VMEM_LIMIT),
        name="out_mlp",
    )(x2, ma, mb, wo, gm, wup, wdn, gf)


def kernel(x, attn_norm_g, w_in, q_norm_g, k_norm_g, rel_bias, out_norm_a_g, out_norm_b_g,
           w_out, mlp_norm_g, w_up, w_down, final_norm_g):
    B, S, D = x.shape
    assert w_in.shape[0] == 1, "the output kernel fuses the final norm: single layer only"
    table = _toeplitz_chunks(_offset_bias(rel_bias, S), S)
    gq = jnp.tile(q_norm_g[0], LANES // HEAD_DIM).reshape(1, LANES)
    gk = jnp.tile(k_norm_g[0], LANES // HEAD_DIM).reshape(1, LANES)
    qa, kat, va, qb, kbt, vb = _in_projection(
        x, attn_norm_g[0].reshape(1, D), w_in[0].astype(BF16), gq, gk)
    ma = _attention_a(qa, kat, va, out_norm_a_g[0].reshape(1, A_WIDTH))
    mb = _attention_b(qb, kbt, vb, table, out_norm_b_g[0].reshape(1, B_WIDTH))
    out = _out_mlp(x.reshape(B * S, D), ma.reshape(B * S, A_WIDTH), mb.reshape(B * S, B_WIDTH),
                   w_out[0].astype(BF16), mlp_norm_g[0].reshape(1, D),
                   w_up[0].astype(BF16), w_down[0].astype(BF16), final_norm_g.reshape(1, D))
    return out.reshape(B, S, D)
```

```python
import functools
import math

import jax
import jax.numpy as jnp
import numpy as np
from jax import lax
from jax.experimental import pallas as pl
from jax.experimental.pallas import tpu as pltpu

HEAD_DIM = 64
A_Q_HEADS = 8
A_KV_HEADS = 2
B_HEADS = 8
GRID_W = 64
ROPE_THETA = 10000.0
DIL_BRANCHES = ((128, 1), (512, 4), (2048, 16))
N_BUCKETS = 32
BUCKET_MAX_DIST = 1024
EPS = 1e-6
NEG = -1e30

A_WIDTH = A_Q_HEADS * HEAD_DIM
A_KV_WIDTH = A_KV_HEADS * HEAD_DIM
B_WIDTH = B_HEADS * HEAD_DIM
Q_SCALE = HEAD_DIM ** -0.5
QA_SCALE = Q_SCALE * math.log2(math.e)

LANES = 128
TM_IN = 512
TQ = 128
TQ_A = 256
TM_OUT = 512
FF_CHUNK = 1024
VMEM_LIMIT = 56 * 1024 * 1024

F32 = jnp.float32
BF16 = jnp.bfloat16


def _rms(x, g):
    return x * lax.rsqrt(jnp.mean(x * x, axis=-1, keepdims=True) + EPS) * g


def _inproj_kernel(x_ref, g_ref, w_ref, gq_ref, gk_ref, cos_ref, sin_ref, ones_ref,
                   qat_ref, ka_ref, vat_ref, qb_ref, kbt_ref, vb_ref):
    n = _rms(x_ref[...], g_ref[...]).astype(BF16)
    cos = cos_ref[...]
    sin = sin_ref[...]
    ones2 = ones_ref[...]
    lane = lax.broadcasted_iota(jnp.int32, (TM_IN, LANES), 1)
    first_half = (lane % 32) < 16

    def head_norm_rope(blk, gain):
        sq = blk * blk
        hi = sq.astype(BF16)
        lo = (sq - hi.astype(F32)).astype(BF16)
        ss = jnp.dot(jnp.concatenate([hi, lo], axis=1), ones2, preferred_element_type=F32)
        xn = blk * lax.rsqrt(ss * (1.0 / HEAD_DIM) + EPS) * gain
        partner = jnp.where(first_half, pltpu.roll(xn, LANES - 16, 1), pltpu.roll(xn, 16, 1))
        return xn * cos + partner * sin

    o_ka = A_WIDTH
    o_va = o_ka + A_KV_WIDTH
    o_qb = o_va + A_KV_WIDTH
    o_kb = o_qb + B_WIDTH
    o_vb = o_kb + B_WIDTH

    pq = jnp.dot(n, w_ref[:, 0:o_ka], preferred_element_type=F32)
    gq = gq_ref[...]
    for j in range(A_WIDTH // LANES):
        sl = slice(j * LANES, (j + 1) * LANES)
        qat_ref[sl, :] = (head_norm_rope(pq[:, sl], gq) * QA_SCALE).T.astype(BF16)

    pk = jnp.dot(n, w_ref[:, o_ka:o_va], preferred_element_type=F32)
    kr = head_norm_rope(pk, gk_ref[...])
    swapped = pltpu.roll(kr, HEAD_DIM, 1)
    low = lane < HEAD_DIM
    ka_ref[:, 0:LANES] = jnp.where(low, kr, swapped).astype(BF16)
    ka_ref[:, LANES:2 * LANES] = jnp.where(low, swapped, kr).astype(BF16)

    pv = jnp.dot(n, w_ref[:, o_va:o_qb], preferred_element_type=F32)
    vat_ref[...] = pv.T.astype(BF16)

    qb_ref[...] = (jnp.dot(n, w_ref[:, o_qb:o_kb], preferred_element_type=F32) * Q_SCALE).astype(BF16)
    pkb = jnp.dot(n, w_ref[:, o_kb:o_vb], preferred_element_type=F32)
    for j in range(B_WIDTH // LANES):
        sl = slice(j * LANES, (j + 1) * LANES)
        kbt_ref[sl, :] = pkb[:, sl].T.astype(BF16)
    vb_ref[...] = jnp.dot(n, w_ref[:, o_vb:], preferred_element_type=F32).astype(BF16)


def _rope_tables(S):
    rows = S // GRID_W
    row = jnp.repeat(jnp.arange(rows, dtype=F32), GRID_W)
    col = jnp.tile(jnp.arange(GRID_W, dtype=F32), rows)
    n_freq = HEAD_DIM // 4
    inv = ROPE_THETA ** (-jnp.arange(n_freq, dtype=F32) / n_freq)
    ang_r = row[:, None] * inv[None, :]
    ang_c = col[:, None] * inv[None, :]
    cos_h = jnp.concatenate([jnp.cos(ang_r)] * 2 + [jnp.cos(ang_c)] * 2, axis=1)
    sin_h = jnp.concatenate([-jnp.sin(ang_r), jnp.sin(ang_r), -jnp.sin(ang_c), jnp.sin(ang_c)], axis=1)
    return jnp.tile(cos_h, (1, 2)), jnp.tile(sin_h, (1, 2))


def _in_projection(x, g, w_bf16, gq, gk):
    B, S, D = x.shape
    in_width = w_bf16.shape[1]
    cos, sin = _rope_tables(S)
    r = np.arange(2 * LANES)[:, None]
    c = np.arange(LANES)[None, :]
    ones2 = jnp.asarray(((r % LANES) // HEAD_DIM) == (c // HEAD_DIM), dtype=BF16)
    nt = S // TM_IN
    tok = lambda t, b: (b, t, 0)
    const = lambda t, b: (0, 0)
    out_shape = (
        jax.ShapeDtypeStruct((B, A_WIDTH, S), BF16),
        jax.ShapeDtypeStruct((B, S, 2 * A_KV_WIDTH), BF16),
        jax.ShapeDtypeStruct((B, A_KV_WIDTH, S), BF16),
        jax.ShapeDtypeStruct((B, S, B_WIDTH), BF16),
        jax.ShapeDtypeStruct((B, B_WIDTH, S), BF16),
        jax.ShapeDtypeStruct((B, S, B_WIDTH), BF16),
    )
    return pl.pallas_call(
        _inproj_kernel,
        grid=(nt, B),
        in_specs=[
            pl.BlockSpec((None, TM_IN, D), tok),
            pl.BlockSpec((1, D), const),
            pl.BlockSpec((D, in_width), const),
            pl.BlockSpec((1, LANES), const),
            pl.BlockSpec((1, LANES), const),
            pl.BlockSpec((TM_IN, LANES), lambda t, b: (t, 0)),
            pl.BlockSpec((TM_IN, LANES), lambda t, b: (t, 0)),
            pl.BlockSpec((2 * LANES, LANES), const),
        ],
        out_specs=(
            pl.BlockSpec((None, A_WIDTH, TM_IN), lambda t, b: (b, 0, t)),
            pl.BlockSpec((None, TM_IN, 2 * A_KV_WIDTH), tok),
            pl.BlockSpec((None, A_KV_WIDTH, TM_IN), lambda t, b: (b, 0, t)),
            pl.BlockSpec((None, TM_IN, B_WIDTH), tok),
            pl.BlockSpec((None, B_WIDTH, TM_IN), lambda t, b: (b, 0, t)),
            pl.BlockSpec((None, TM_IN, B_WIDTH), tok),
        ),
        out_shape=out_shape,
        compiler_params=pltpu.CompilerParams(
            dimension_semantics=("arbitrary", "arbitrary"), vmem_limit_bytes=VMEM_LIMIT),
        name="in_projection",
    )(x, g, w_bf16, gq, gk, cos, sin, ones2)


def _split_pair(q2):
    lane = lax.broadcasted_iota(jnp.int32, q2.shape, 1)
    low = lane < HEAD_DIM
    zero = jnp.zeros_like(q2)
    return jnp.concatenate([jnp.where(low, q2, zero), jnp.where(low, zero, q2)], axis=0)


def _softmax_pv(s, v2):
    m = jnp.max(s, axis=-1, keepdims=True)
    p = jnp.exp(s - m)
    l = jnp.sum(p, axis=-1, keepdims=True)
    o = jnp.dot(p.astype(BF16), v2, preferred_element_type=F32)
    return o * (1.0 / l)


def _merge_pair(o2):
    t = o2.shape[0] // 2
    lane = lax.broadcasted_iota(jnp.int32, (t, LANES), 1)
    return jnp.where(lane < HEAD_DIM, o2[0:t], o2[t:])


def _attn_a_kernel(qt_ref, k_ref, vt_ref, g_ref, o_ref):
    group = A_Q_HEADS // A_KV_HEADS
    S = k_ref.shape[0]
    top = lax.broadcasted_iota(jnp.int32, (LANES, TQ_A), 0) < HEAD_DIM
    ones = jnp.ones((16, S), BF16)
    logits = []
    for kv in range(A_KV_HEADS):
        cols = []
        for j in range(group // 2):
            r0 = (kv * group + 2 * j) * HEAD_DIM
            qt2 = qt_ref[r0:r0 + LANES, :]
            zero = jnp.zeros_like(qt2)
            cols += [jnp.where(top, qt2, zero), jnp.where(top, zero, qt2)]
        w = jnp.concatenate(cols, axis=1)
        k2 = k_ref[:, kv * LANES:(kv + 1) * LANES]
        logits.append(jnp.dot(k2, w, preferred_element_type=F32))
    blocks = []
    for kv in range(A_KV_HEADS):
        st = logits[kv]
        p = jnp.exp2(st - jnp.max(st, axis=0, keepdims=True)).astype(BF16)
        vaug = jnp.concatenate([vt_ref[kv * HEAD_DIM:(kv + 1) * HEAD_DIM, :], ones], axis=0)
        oa = jnp.dot(vaug, p, preferred_element_type=F32)
        ot = oa[0:HEAD_DIM] * (1.0 / oa[HEAD_DIM:HEAD_DIM + 1])
        for j in range(group // 2):
            pair_t = jnp.concatenate([ot[:, 2 * j * TQ_A:(2 * j + 1) * TQ_A],
                                      ot[:, (2 * j + 1) * TQ_A:(2 * j + 2) * TQ_A]], axis=0)
            blocks.append(pair_t.T)
    y = jnp.concatenate(blocks, axis=1)
    o_ref[...] = _rms(y, g_ref[...]).astype(BF16)


def _attention_a(qat, ka, vat, g):
    B, _, S = qat.shape
    return pl.pallas_call(
        _attn_a_kernel,
        grid=(B, S // TQ_A),
        in_specs=[
            pl.BlockSpec((None, A_WIDTH, TQ_A), lambda b, i: (b, 0, i)),
            pl.BlockSpec((None, S, 2 * A_KV_WIDTH), lambda b, i: (b, 0, 0)),
            pl.BlockSpec((None, A_KV_WIDTH, S), lambda b, i: (b, 0, 0)),
            pl.BlockSpec((1, A_WIDTH), lambda b, i: (0, 0)),
        ],
        out_specs=pl.BlockSpec((None, TQ_A, A_WIDTH), lambda b, i: (b, i, 0)),
        out_shape=jax.ShapeDtypeStruct((B, S, A_WIDTH), BF16),
        compiler_params=pltpu.CompilerParams(
            dimension_semantics=("arbitrary", "arbitrary"), vmem_limit_bytes=VMEM_LIMIT),
        name="attention_a",
    )(qat, ka, vat, g)


def _t5_bucket(rel):
    nb_half = N_BUCKETS // 2
    max_exact = nb_half // 2
    ret = jnp.where(rel > 0, nb_half, 0)
    n = jnp.abs(rel)
    large = max_exact + (jnp.log(jnp.maximum(n, 1).astype(F32) / max_exact)
                         / math.log(BUCKET_MAX_DIST / max_exact)
                         * (nb_half - max_exact)).astype(jnp.int32)
    large = jnp.minimum(large, nb_half - 1)
    return ret + jnp.where(n < max_exact, n, large)


def _bias_kernel(rbt_ref, code_ref, logc_ref, o_ref):
    code = code_ref[...]
    rbt = rbt_ref[...]
    acc = jnp.zeros(o_ref.shape, F32)
    for b in range(N_BUCKETS):
        acc = jnp.where(code == b, rbt[:, b:b + 1], acc)
    o_ref[...] = acc + logc_ref[...]


def _offset_bias(rel_bias, S):
    d = jnp.arange(2 * S, dtype=jnp.int32) - S
    mult = np.zeros((2 * S,), np.float64)
    dn = np.arange(2 * S) - S
    for window, r in DIL_BRANCHES:
        mult += (dn % r == 0) & (np.abs(dn) <= (window // (2 * r)) * r)
    logc = np.where(mult > 0, np.log(np.maximum(mult, 1.0)), NEG).astype(np.float32)
    code = _t5_bucket(d).reshape(1, 2 * S)
    return pl.pallas_call(
        _bias_kernel,
        out_shape=jax.ShapeDtypeStruct((B_HEADS, 2 * S), F32),
        name="offset_bias",
    )(rel_bias.T, code, jnp.asarray(logc).reshape(1, 2 * S))


def _toeplitz_chunks(g1, S):
    H, W = g1.shape
    v = jnp.concatenate([g1, jnp.full((H, 1), NEG, F32)], axis=1)
    rows = jnp.broadcast_to(v[:, None, :], (H, TQ, W + 1)).reshape(H, TQ * (W + 1))
    rows = rows[:, :TQ * W].reshape(H, TQ, W)
    return rows.reshape(H, TQ, W // LANES, LANES).transpose(0, 2, 1, 3)


def _attn_b_kernel(q_ref, kt_ref, v_ref, t_ref, g_ref, o_ref, *, n_key_chunks, n_q_blocks):
    i = pl.program_id(1)
    base = n_q_blocks - i
    blocks = []
    for j in range(B_HEADS // 2):
        sl = slice(j * LANES, (j + 1) * LANES)
        qs = _split_pair(q_ref[:, sl])
        s = jnp.dot(qs, kt_ref[sl, :], preferred_element_type=F32)
        bias = jnp.concatenate(
            [jnp.concatenate([t_ref[2 * j + h, base + c] for c in range(n_key_chunks)], axis=1)
             for h in range(2)], axis=0)
        blocks.append(_merge_pair(_softmax_pv(s + bias, v_ref[:, sl])))
    y = jnp.concatenate(blocks, axis=1)
    o_ref[...] = _rms(y, g_ref[...]).astype(BF16)


def _attention_b(qb, kbt, vb, table, g):
    B, S, _ = qb.shape
    nq = S // TQ
    kern = functools.partial(_attn_b_kernel, n_key_chunks=S // LANES, n_q_blocks=nq)
    return pl.pallas_call(
        kern,
        grid=(B, nq),
        in_specs=[
            pl.BlockSpec((None, TQ, B_WIDTH), lambda b, i: (b, i, 0)),
            pl.BlockSpec((None, B_WIDTH, S), lambda b, i: (b, 0, 0)),
            pl.BlockSpec((None, S, B_WIDTH), lambda b, i: (b, 0, 0)),
            pl.BlockSpec(table.shape, lambda b, i: (0, 0, 0, 0), pipeline_mode=pl.Buffered(1)),
            pl.BlockSpec((1, B_WIDTH), lambda b, i: (0, 0)),
        ],
        out_specs=pl.BlockSpec((None, TQ, B_WIDTH), lambda b, i: (b, i, 0)),
        out_shape=jax.ShapeDtypeStruct((B, S, B_WIDTH), BF16),
        compiler_params=pltpu.CompilerParams(
            dimension_semantics=("arbitrary", "arbitrary"), vmem_limit_bytes=VMEM_LIMIT),
        name="attention_b",
    )(qb, kbt, vb, table, g)


def _out_mlp_kernel(x_ref, ma_ref, mb_ref, wo_ref, gm_ref, wup_ref, wdn_ref, gf_ref, o_ref):
    mix = jnp.concatenate([ma_ref[...], mb_ref[...]], axis=1)
    h = x_ref[...] + jnp.dot(mix, wo_ref[...], preferred_element_type=F32)
    hn = _rms(h, gm_ref[...]).astype(BF16)
    acc = h
    d_ff = wup_ref.shape[1]
    for c in range(d_ff // FF_CHUNK):
        sl = slice(c * FF_CHUNK, (c + 1) * FF_CHUNK)
        u = jnp.maximum(jnp.dot(hn, wup_ref[:, sl], preferred_element_type=F32), 0.0)
        acc = acc + jnp.dot((u * u).astype(BF16), wdn_ref[sl, :], preferred_element_type=F32)
    o_ref[...] = _rms(acc, gf_ref[...])


def _out_mlp(x2, ma, mb, wo, gm, wup, wdn, gf):
    N, D = x2.shape
    d_ff = wup.shape[1]
    tok = lambda t: (t, 0)
    const = lambda t: (0, 0)
    resident = functools.partial(pl.BlockSpec, index_map=const, pipeline_mode=pl.Buffered(1))
    return pl.pallas_call(
        _out_mlp_kernel,
        grid=(N // TM_OUT,),
        in_specs=[
            pl.BlockSpec((TM_OUT, D), tok),
            pl.BlockSpec((TM_OUT, A_WIDTH), tok),
            pl.BlockSpec((TM_OUT, B_WIDTH), tok),
            resident((A_WIDTH + B_WIDTH, D)),
            pl.BlockSpec((1, D), const),
            resident((D, d_ff)),
            resident((d_ff, D)),
            pl.BlockSpec((1, D), const),
        ],
        out_specs=pl.BlockSpec((TM_OUT, D), tok),
        out_shape=jax.ShapeDtypeStruct((N, D), F32),
        compiler_params=pltpu.CompilerParams(
            dimension_semantics=("arbitrary",), vmem_limit_bytes=VMEM_LIMIT),
        name="out_mlp",
    )(x2, ma, mb, wo, gm, wup, wdn, gf)


def kernel(x, attn_norm_g, w_in, q_norm_g, k_norm_g, rel_bias, out_norm_a_g, out_norm_b_g,
           w_out, mlp_norm_g, w_up, w_down, final_norm_g):
    B, S, D = x.shape
    assert w_in.shape[0] == 1, "the output kernel fuses the final norm: single layer only"
    table = _toeplitz_chunks(_offset_bias(rel_bias, S), S)
    gq = jnp.tile(q_norm_g[0], LANES // HEAD_DIM).reshape(1, LANES)
    gk = jnp.tile(k_norm_g[0], LANES // HEAD_DIM).reshape(1, LANES)
    qat, ka, vat, qb, kbt, vb = _in_projection(
        x, attn_norm_g[0].reshape(1, D), w_in[0].astype(BF16), gq, gk)
    ma = _attention_a(qat, ka, vat, out_norm_a_g[0].reshape(1, A_WIDTH))
    mb = _attention_b(qb, kbt, vb, table, out_norm_b_g[0].reshape(1, B_WIDTH))
    out = _out_mlp(x.reshape(B * S, D), ma.reshape(B * S, A_WIDTH), mb.reshape(B * S, B_WIDTH),
                   w_out[0].astype(BF16), mlp_norm_g[0].reshape(1, D),
                   w_up[0].astype(BF16), w_down[0].astype(BF16), final_norm_g.reshape(1, D))
    return out.reshape(B, S, D)
```

```python
import functools
import math

import jax
import jax.numpy as jnp
import numpy as np
from jax import lax
from jax.experimental import pallas as pl
from jax.experimental.pallas import tpu as pltpu

HEAD_DIM = 64
A_Q_HEADS = 8
A_KV_HEADS = 2
B_HEADS = 8
GRID_W = 64
ROPE_THETA = 10000.0
DIL_BRANCHES = ((128, 1), (512, 4), (2048, 16))
N_BUCKETS = 32
BUCKET_MAX_DIST = 1024
EPS = 1e-6
NEG = -1e30

A_WIDTH = A_Q_HEADS * HEAD_DIM
A_KV_WIDTH = A_KV_HEADS * HEAD_DIM
B_WIDTH = B_HEADS * HEAD_DIM
Q_SCALE = HEAD_DIM ** -0.5
QA_SCALE = Q_SCALE * math.log2(math.e)

LANES = 128
TM_IN = 512
TQ = 128
TQ_A = 256
TM_OUT = 512
FF_CHUNK = 1024
VMEM_LIMIT = 56 * 1024 * 1024

F32 = jnp.float32
BF16 = jnp.bfloat16


def _rms(x, g):
    return x * lax.rsqrt(jnp.mean(x * x, axis=-1, keepdims=True) + EPS) * g


def _inproj_kernel(x_ref, g_ref, w_ref, gq_ref, gk_ref, cos_ref, sin_ref, ones_ref,
                   qat_ref, ka_ref, vat_ref, qb_ref, kbt_ref, vb_ref):
    n = _rms(x_ref[...], g_ref[...]).astype(BF16)
    cos = cos_ref[...]
    sin = sin_ref[...]
    ones2 = ones_ref[...]
    lane = lax.broadcasted_iota(jnp.int32, (TM_IN, LANES), 1)
    first_half = (lane % 32) < 16

    def head_norm_rope(blk, gain):
        sq = blk * blk
        hi = sq.astype(BF16)
        lo = (sq - hi.astype(F32)).astype(BF16)
        ss = jnp.dot(jnp.concatenate([hi, lo], axis=1), ones2, preferred_element_type=F32)
        xn = blk * lax.rsqrt(ss * (1.0 / HEAD_DIM) + EPS) * gain
        partner = jnp.where(first_half, pltpu.roll(xn, LANES - 16, 1), pltpu.roll(xn, 16, 1))
        return xn * cos + partner * sin

    o_ka = A_WIDTH
    o_va = o_ka + A_KV_WIDTH
    o_qb = o_va + A_KV_WIDTH
    o_kb = o_qb + B_WIDTH
    o_vb = o_kb + B_WIDTH

    pq = jnp.dot(n, w_ref[:, 0:o_ka], preferred_element_type=F32)
    gq = gq_ref[...]
    for j in range(A_WIDTH // LANES):
        sl = slice(j * LANES, (j + 1) * LANES)
        qat_ref[sl, :] = (head_norm_rope(pq[:, sl], gq) * QA_SCALE).T.astype(BF16)

    pk = jnp.dot(n, w_ref[:, o_ka:o_va], preferred_element_type=F32)
    kr = head_norm_rope(pk, gk_ref[...])
    swapped = pltpu.roll(kr, HEAD_DIM, 1)
    low = lane < HEAD_DIM
    ka_ref[:, 0:LANES] = jnp.where(low, kr, swapped).astype(BF16)
    ka_ref[:, LANES:2 * LANES] = jnp.where(low, swapped, kr).astype(BF16)

    pv = jnp.dot(n, w_ref[:, o_va:o_qb], preferred_element_type=F32)
    vat_ref[...] = pv.T.astype(BF16)

    qb_ref[...] = (jnp.dot(n, w_ref[:, o_qb:o_kb], preferred_element_type=F32) * Q_SCALE).astype(BF16)
    pkb = jnp.dot(n, w_ref[:, o_kb:o_vb], preferred_element_type=F32)
    for j in range(B_WIDTH // LANES):
        sl = slice(j * LANES, (j + 1) * LANES)
        kbt_ref[sl, :] = pkb[:, sl].T.astype(BF16)
    vb_ref[...] = jnp.dot(n, w_ref[:, o_vb:], preferred_element_type=F32).astype(BF16)


def _rope_tables(S):
    rows = S // GRID_W
    row = jnp.repeat(jnp.arange(rows, dtype=F32), GRID_W)
    col = jnp.tile(jnp.arange(GRID_W, dtype=F32), rows)
    n_freq = HEAD_DIM // 4
    inv = ROPE_THETA ** (-jnp.arange(n_freq, dtype=F32) / n_freq)
    ang_r = row[:, None] * inv[None, :]
    ang_c = col[:, None] * inv[None, :]
    cos_h = jnp.concatenate([jnp.cos(ang_r)] * 2 + [jnp.cos(ang_c)] * 2, axis=1)
    sin_h = jnp.concatenate([-jnp.sin(ang_r), jnp.sin(ang_r), -jnp.sin(ang_c), jnp.sin(ang_c)], axis=1)
    return jnp.tile(cos_h, (1, 2)), jnp.tile(sin_h, (1, 2))


def _in_projection(x, g, w_bf16, gq, gk):
    B, S, D = x.shape
    in_width = w_bf16.shape[1]
    cos, sin = _rope_tables(S)
    r = np.arange(2 * LANES)[:, None]
    c = np.arange(LANES)[None, :]
    ones2 = jnp.asarray(((r % LANES) // HEAD_DIM) == (c // HEAD_DIM), dtype=BF16)
    nt = S // TM_IN
    tok = lambda t, b: (b, t, 0)
    const = lambda t, b: (0, 0)
    out_shape = (
        jax.ShapeDtypeStruct((B, A_WIDTH, S), BF16),
        jax.ShapeDtypeStruct((B, S, 2 * A_KV_WIDTH), BF16),
        jax.ShapeDtypeStruct((B, A_KV_WIDTH, S), BF16),
        jax.ShapeDtypeStruct((B, S, B_WIDTH), BF16),
        jax.ShapeDtypeStruct((B, B_WIDTH, S), BF16),
        jax.ShapeDtypeStruct((B, S, B_WIDTH), BF16),
    )
    return pl.pallas_call(
        _inproj_kernel,
        grid=(nt, B),
        in_specs=[
            pl.BlockSpec((None, TM_IN, D), tok),
            pl.BlockSpec((1, D), const),
            pl.BlockSpec((D, in_width), const),
            pl.BlockSpec((1, LANES), const),
            pl.BlockSpec((1, LANES), const),
            pl.BlockSpec((TM_IN, LANES), lambda t, b: (t, 0)),
            pl.BlockSpec((TM_IN, LANES), lambda t, b: (t, 0)),
            pl.BlockSpec((2 * LANES, LANES), const),
        ],
        out_specs=(
            pl.BlockSpec((None, A_WIDTH, TM_IN), lambda t, b: (b, 0, t)),
            pl.BlockSpec((None, TM_IN, 2 * A_KV_WIDTH), tok),
            pl.BlockSpec((None, A_KV_WIDTH, TM_IN), lambda t, b: (b, 0, t)),
            pl.BlockSpec((None, TM_IN, B_WIDTH), tok),
            pl.BlockSpec((None, B_WIDTH, TM_IN), lambda t, b: (b, 0, t)),
            pl.BlockSpec((None, TM_IN, B_WIDTH), tok),
        ),
        out_shape=out_shape,
        compiler_params=pltpu.CompilerParams(
            dimension_semantics=("arbitrary", "arbitrary"), vmem_limit_bytes=VMEM_LIMIT),
        name="in_projection",
    )(x, g, w_bf16, gq, gk, cos, sin, ones2)


def _split_pair(q2):
    lane = lax.broadcasted_iota(jnp.int32, q2.shape, 1)
    low = lane < HEAD_DIM
    zero = jnp.zeros_like(q2)
    return jnp.concatenate([jnp.where(low, q2, zero), jnp.where(low, zero, q2)], axis=0)


def _softmax_pv(s, v2):
    m = jnp.max(s, axis=-1, keepdims=True)
    p = jnp.exp(s - m)
    l = jnp.sum(p, axis=-1, keepdims=True)
    o = jnp.dot(p.astype(BF16), v2, preferred_element_type=F32)
    return o * (1.0 / l)


def _merge_pair(o2):
    t = o2.shape[0] // 2
    lane = lax.broadcasted_iota(jnp.int32, (t, LANES), 1)
    return jnp.where(lane < HEAD_DIM, o2[0:t], o2[t:])


def _attn_a_kernel(qt_ref, k_ref, vt_ref, g_ref, o_ref):
    group = A_Q_HEADS // A_KV_HEADS
    S = k_ref.shape[0]
    top = lax.broadcasted_iota(jnp.int32, (LANES, TQ_A), 0) < HEAD_DIM
    ones = jnp.ones((16, S), BF16)
    logits = []
    for kv in range(A_KV_HEADS):
        cols = []
        for j in range(group // 2):
            r0 = (kv * group + 2 * j) * HEAD_DIM
            qt2 = qt_ref[r0:r0 + LANES, :]
            zero = jnp.zeros_like(qt2)
            cols += [jnp.where(top, qt2, zero), jnp.where(top, zero, qt2)]
        w = jnp.concatenate(cols, axis=1)
        k2 = k_ref[:, kv * LANES:(kv + 1) * LANES]
        logits.append(jnp.dot(k2, w, preferred_element_type=F32))
    blocks = []
    for kv in range(A_KV_HEADS):
        st = logits[kv]
        p = jnp.exp2(st - jnp.max(st, axis=0, keepdims=True)).astype(BF16)
        vaug = jnp.concatenate([vt_ref[kv * HEAD_DIM:(kv + 1) * HEAD_DIM, :], ones], axis=0)
        oa = jnp.dot(vaug, p, preferred_element_type=F32)
        ot = oa[0:HEAD_DIM] * (1.0 / oa[HEAD_DIM:HEAD_DIM + 1])
        for j in range(group // 2):
            pair_t = jnp.concatenate([ot[:, 2 * j * TQ_A:(2 * j + 1) * TQ_A],
                                      ot[:, (2 * j + 1) * TQ_A:(2 * j + 2) * TQ_A]], axis=0)
            blocks.append(pair_t.T)
    y = jnp.concatenate(blocks, axis=1)
    o_ref[...] = _rms(y, g_ref[...]).astype(BF16)


def _attention_a(qat, ka, vat, g):
    B, _, S = qat.shape
    return pl.pallas_call(
        _attn_a_kernel,
        grid=(B, S // TQ_A),
        in_specs=[
            pl.BlockSpec((None, A_WIDTH, TQ_A), lambda b, i: (b, 0, i)),
            pl.BlockSpec((None, S, 2 * A_KV_WIDTH), lambda b, i: (b, 0, 0)),
            pl.BlockSpec((None, A_KV_WIDTH, S), lambda b, i: (b, 0, 0)),
            pl.BlockSpec((1, A_WIDTH), lambda b, i: (0, 0)),
        ],
        out_specs=pl.BlockSpec((None, TQ_A, A_WIDTH), lambda b, i: (b, i, 0)),
        out_shape=jax.ShapeDtypeStruct((B, S, A_WIDTH), BF16),
        compiler_params=pltpu.CompilerParams(
            dimension_semantics=("arbitrary", "arbitrary"), vmem_limit_bytes=VMEM_LIMIT),
        name="attention_a",
    )(qat, ka, vat, g)


def _t5_bucket(rel):
    nb_half = N_BUCKETS // 2
    max_exact = nb_half // 2
    ret = jnp.where(rel > 0, nb_half, 0)
    n = jnp.abs(rel)
    large = max_exact + (jnp.log(jnp.maximum(n, 1).astype(F32) / max_exact)
                         / math.log(BUCKET_MAX_DIST / max_exact)
                         * (nb_half - max_exact)).astype(jnp.int32)
    large = jnp.minimum(large, nb_half - 1)
    return ret + jnp.where(n < max_exact, n, large)


def _bias_kernel(rbt_ref, code_ref, logc_ref, o_ref):
    h = pl.program_id(0)
    code = code_ref[...]
    rb = rbt_ref[pl.ds(h, 1), :]
    acc = jnp.zeros(code.shape, F32)
    for b in range(N_BUCKETS):
        acc = jnp.where(code == b, rb[:, b:b + 1], acc)
    g1 = acc + logc_ref[...]
    rows = pltpu.roll(jnp.broadcast_to(g1, (TQ, g1.shape[1])), 0, 1, stride=1, stride_axis=0)
    for c in range(o_ref.shape[0]):
        o_ref[c] = rows[:, c * LANES:(c + 1) * LANES]


def _offset_bias_table(rel_bias, S):
    d = jnp.arange(2 * S, dtype=jnp.int32) - S
    mult = np.zeros((2 * S,), np.float64)
    dn = np.arange(2 * S) - S
    for window, r in DIL_BRANCHES:
        mult += (dn % r == 0) & (np.abs(dn) <= (window // (2 * r)) * r)
    logc = np.where(mult > 0, np.log(np.maximum(mult, 1.0)), NEG).astype(np.float32)
    code = _t5_bucket(d).reshape(1, 2 * S)
    n_chunks = 2 * S // LANES
    const = lambda h: (0, 0)
    return pl.pallas_call(
        _bias_kernel,
        grid=(B_HEADS,),
        in_specs=[pl.BlockSpec((B_HEADS, N_BUCKETS), const),
                  pl.BlockSpec((1, 2 * S), const),
                  pl.BlockSpec((1, 2 * S), const)],
        out_specs=pl.BlockSpec((None, n_chunks, TQ, LANES), lambda h: (h, 0, 0, 0)),
        out_shape=jax.ShapeDtypeStruct((B_HEADS, n_chunks, TQ, LANES), F32),
        compiler_params=pltpu.CompilerParams(dimension_semantics=("arbitrary",)),
        name="offset_bias",
    )(rel_bias.T, code, jnp.asarray(logc).reshape(1, 2 * S))


def _attn_b_kernel(q_ref, kt_ref, v_ref, t_ref, g_ref, o_ref, *, n_key_chunks, n_q_blocks):
    i = pl.program_id(1)
    base = n_q_blocks - i
    blocks = []
    for j in range(B_HEADS // 2):
        sl = slice(j * LANES, (j + 1) * LANES)
        qs = _split_pair(q_ref[:, sl])
        s = jnp.dot(qs, kt_ref[sl, :], preferred_element_type=F32)
        bias = jnp.concatenate(
            [jnp.concatenate([t_ref[2 * j + h, base + c] for c in range(n_key_chunks)], axis=1)
             for h in range(2)], axis=0)
        blocks.append(_merge_pair(_softmax_pv(s + bias, v_ref[:, sl])))
    y = jnp.concatenate(blocks, axis=1)
    o_ref[...] = _rms(y, g_ref[...]).astype(BF16)


def _attention_b(qb, kbt, vb, table, g):
    B, S, _ = qb.shape
    nq = S // TQ
    kern = functools.partial(_attn_b_kernel, n_key_chunks=S // LANES, n_q_blocks=nq)
    return pl.pallas_call(
        kern,
        grid=(B, nq),
        in_specs=[
            pl.BlockSpec((None, TQ, B_WIDTH), lambda b, i: (b, i, 0)),
            pl.BlockSpec((None, B_WIDTH, S), lambda b, i: (b, 0, 0)),
            pl.BlockSpec((None, S, B_WIDTH), lambda b, i: (b, 0, 0)),
            pl.BlockSpec(table.shape, lambda b, i: (0, 0, 0, 0), pipeline_mode=pl.Buffered(1)),
            pl.BlockSpec((1, B_WIDTH), lambda b, i: (0, 0)),
        ],
        out_specs=pl.BlockSpec((None, TQ, B_WIDTH), lambda b, i: (b, i, 0)),
        out_shape=jax.ShapeDtypeStruct((B, S, B_WIDTH), BF16),
        compiler_params=pltpu.CompilerParams(
            dimension_semantics=("arbitrary", "arbitrary"), vmem_limit_bytes=VMEM_LIMIT),
        name="attention_b",
    )(qb, kbt, vb, table, g)


def _out_mlp_kernel(x_ref, ma_ref, mb_ref, wo_ref, gm_ref, wup_ref, wdn_ref, gf_ref, o_ref):
    mix = jnp.concatenate([ma_ref[...], mb_ref[...]], axis=1)
    h = x_ref[...] + jnp.dot(mix, wo_ref[...], preferred_element_type=F32)
    hn = _rms(h, gm_ref[...]).astype(BF16)
    acc = h
    d_ff = wup_ref.shape[1]
    for c in range(d_ff // FF_CHUNK):
        sl = slice(c * FF_CHUNK, (c + 1) * FF_CHUNK)
        u = jnp.maximum(jnp.dot(hn, wup_ref[:, sl], preferred_element_type=F32), 0.0)
        acc = acc + jnp.dot((u * u).astype(BF16), wdn_ref[sl, :], preferred_element_type=F32)
    o_ref[...] = _rms(acc, gf_ref[...])


def _out_mlp(x2, ma, mb, wo, gm, wup, wdn, gf):
    N, D = x2.shape
    d_ff = wup.shape[1]
    tok = lambda t: (t, 0)
    const = lambda t: (0, 0)
    resident = functools.partial(pl.BlockSpec, index_map=const, pipeline_mode=pl.Buffered(1))
    return pl.pallas_call(
        _out_mlp_kernel,
        grid=(N // TM_OUT,),
        in_specs=[
            pl.BlockSpec((TM_OUT, D), tok),
            pl.BlockSpec((TM_OUT, A_WIDTH), tok),
            pl.BlockSpec((TM_OUT, B_WIDTH), tok),
            resident((A_WIDTH + B_WIDTH, D)),
            pl.BlockSpec((1, D), const),
            resident((D, d_ff)),
            resident((d_ff, D)),
            pl.BlockSpec((1, D), const),
        ],
        out_specs=pl.BlockSpec((TM_OUT, D), tok),
        out_shape=jax.ShapeDtypeStruct((N, D), F32),
        compiler_params=pltpu.CompilerParams(
            dimension_semantics=("arbitrary",), vmem_limit_bytes=VMEM_LIMIT),
        name="out_mlp",
    )(x2, ma, mb, wo, gm, wup, wdn, gf)


def kernel(x, attn_norm_g, w_in, q_norm_g, k_norm_g, rel_bias, out_norm_a_g, out_norm_b_g,
           w_out, mlp_norm_g, w_up, w_down, final_norm_g):
    B, S, D = x.shape
    assert w_in.shape[0] == 1, "the output kernel fuses the final norm: single layer only"
    table = _offset_bias_table(rel_bias, S)
    gq = jnp.tile(q_norm_g[0], LANES // HEAD_DIM).reshape(1, LANES)
    gk = jnp.tile(k_norm_g[0], LANES // HEAD_DIM).reshape(1, LANES)
    qat, ka, vat, qb, kbt, vb = _in_projection(
        x, attn_norm_g[0].reshape(1, D), w_in[0].astype(BF16), gq, gk)
    ma = _attention_a(qat, ka, vat, out_norm_a_g[0].reshape(1, A_WIDTH))
    mb = _attention_b(qb, kbt, vb, table, out_norm_b_g[0].reshape(1, B_WIDTH))
    out = _out_mlp(x.reshape(B * S, D), ma.reshape(B * S, A_WIDTH), mb.reshape(B * S, B_WIDTH),
                   w_out[0].astype(BF16), mlp_norm_g[0].reshape(1, D),
                   w_up[0].astype(BF16), w_down[0].astype(BF16), final_norm_g.reshape(1, D))
    return out.reshape(B, S, D)
```

```python
import functools
import math

import jax
import jax.numpy as jnp
import numpy as np
from jax import lax
from jax.experimental import pallas as pl
from jax.experimental.pallas import tpu as pltpu

HEAD_DIM = 64
A_Q_HEADS = 8
A_KV_HEADS = 2
B_HEADS = 8
GRID_W = 64
ROPE_THETA = 10000.0
DIL_BRANCHES = ((128, 1), (512, 4), (2048, 16))
N_BUCKETS = 32
BUCKET_MAX_DIST = 1024
EPS = 1e-6
NEG = -1e30

A_WIDTH = A_Q_HEADS * HEAD_DIM
A_KV_WIDTH = A_KV_HEADS * HEAD_DIM
B_WIDTH = B_HEADS * HEAD_DIM
LOG2E = math.log2(math.e)
Q_SCALE = HEAD_DIM ** -0.5 * LOG2E

LANES = 128
TM_IN = 512
TQ = 128
TQ_A = 256
TM_OUT = 512
FF_CHUNK = 1024
VMEM_LIMIT = 56 * 1024 * 1024

HALF_WINDOW = DIL_BRANCHES[0][0] // (2 * DIL_BRANCHES[0][1])
assert all(w // (2 * r) == HALF_WINDOW for w, r in DIL_BRANCHES)
KEY_WINDOW = TQ + 2 * HALF_WINDOW
BIAS_SPAN = 2 * KEY_WINDOW
N_VARIANTS = 3
B_UNROLL = 2

F32 = jnp.float32
BF16 = jnp.bfloat16


def _rms(x, g):
    return x * lax.rsqrt(jnp.mean(x * x, axis=-1, keepdims=True) + EPS) * g


def _inproj_kernel(x_ref, g_ref, w_ref, gq_ref, gk_ref, cos_ref, sin_ref, ones_ref,
                   qat_ref, ka_ref, vat_ref,
                   q1_ref, q4_ref, q16_ref, k1_ref, k4_ref, k16_ref, v1_ref, v4_ref, v16_ref,
                   sq_ref, sk_ref, sv_ref):
    n = _rms(x_ref[...], g_ref[...]).astype(BF16)
    cos = cos_ref[...]
    sin = sin_ref[...]
    ones2 = ones_ref[...]
    lane = lax.broadcasted_iota(jnp.int32, (TM_IN, LANES), 1)
    first_half = (lane % 32) < 16

    def head_norm_rope(blk, gain):
        sq = blk * blk
        hi = sq.astype(BF16)
        lo = (sq - hi.astype(F32)).astype(BF16)
        ss = jnp.dot(jnp.concatenate([hi, lo], axis=1), ones2, preferred_element_type=F32)
        xn = blk * lax.rsqrt(ss * (1.0 / HEAD_DIM) + EPS) * gain
        partner = jnp.where(first_half, pltpu.roll(xn, LANES - 16, 1), pltpu.roll(xn, 16, 1))
        return xn * cos + partner * sin

    o_ka = A_WIDTH
    o_va = o_ka + A_KV_WIDTH
    o_qb = o_va + A_KV_WIDTH
    o_kb = o_qb + B_WIDTH
    o_vb = o_kb + B_WIDTH

    pq = jnp.dot(n, w_ref[:, 0:o_ka], preferred_element_type=F32)
    gq = gq_ref[...]
    for j in range(A_WIDTH // LANES):
        sl = slice(j * LANES, (j + 1) * LANES)
        qat_ref[sl, :] = (head_norm_rope(pq[:, sl], gq) * Q_SCALE).T.astype(BF16)

    pk = jnp.dot(n, w_ref[:, o_ka:o_va], preferred_element_type=F32)
    kr = head_norm_rope(pk, gk_ref[...])
    swapped = pltpu.roll(kr, HEAD_DIM, 1)
    low = lane < HEAD_DIM
    ka_ref[:, 0:LANES] = jnp.where(low, kr, swapped).astype(BF16)
    ka_ref[:, LANES:2 * LANES] = jnp.where(low, swapped, kr).astype(BF16)

    pv = jnp.dot(n, w_ref[:, o_va:o_qb], preferred_element_type=F32)
    vat_ref[...] = pv.T.astype(BF16)

    def write_orders(val, scr_ref, nat_ref, ph_refs):
        nat_ref[...] = val.astype(BF16)
        for c in range(B_WIDTH // LANES):
            sl = slice(c * LANES, (c + 1) * LANES)
            scr_ref[c] = val[:, sl]
            for (_, r), ph_ref in zip(DIL_BRANCHES[1:], ph_refs):
                for p in range(r):
                    ph_ref[p, :, sl] = scr_ref[c, pl.ds(p, TM_IN // r, stride=r), :].astype(BF16)

    write_orders(jnp.dot(n, w_ref[:, o_qb:o_kb], preferred_element_type=F32) * Q_SCALE,
                 sq_ref, q1_ref, (q4_ref, q16_ref))
    write_orders(jnp.dot(n, w_ref[:, o_kb:o_vb], preferred_element_type=F32),
                 sk_ref, k1_ref, (k4_ref, k16_ref))
    write_orders(jnp.dot(n, w_ref[:, o_vb:], preferred_element_type=F32),
                 sv_ref, v1_ref, (v4_ref, v16_ref))


def _rope_tables(S):
    rows = S // GRID_W
    row = jnp.repeat(jnp.arange(rows, dtype=F32), GRID_W)
    col = jnp.tile(jnp.arange(GRID_W, dtype=F32), rows)
    n_freq = HEAD_DIM // 4
    inv = ROPE_THETA ** (-jnp.arange(n_freq, dtype=F32) / n_freq)
    ang_r = row[:, None] * inv[None, :]
    ang_c = col[:, None] * inv[None, :]
    cos_h = jnp.concatenate([jnp.cos(ang_r)] * 2 + [jnp.cos(ang_c)] * 2, axis=1)
    sin_h = jnp.concatenate([-jnp.sin(ang_r), jnp.sin(ang_r), -jnp.sin(ang_c), jnp.sin(ang_c)], axis=1)
    return jnp.tile(cos_h, (1, 2)), jnp.tile(sin_h, (1, 2))


def _in_projection(x, g, w_bf16, gq, gk):
    B, S, D = x.shape
    in_width = w_bf16.shape[1]
    cos, sin = _rope_tables(S)
    r = np.arange(2 * LANES)[:, None]
    c = np.arange(LANES)[None, :]
    ones2 = jnp.asarray(((r % LANES) // HEAD_DIM) == (c // HEAD_DIM), dtype=BF16)
    nt = S // TM_IN
    tok = lambda t, b: (b, t, 0)
    phase = lambda t, b: (b, 0, t, 0)
    const = lambda t, b: (0, 0)
    orders_shape, orders_spec = [], []
    for _ in range(3):
        orders_shape.append(jax.ShapeDtypeStruct((B, S, B_WIDTH), BF16))
        orders_spec.append(pl.BlockSpec((None, TM_IN, B_WIDTH), tok))
        for _, rr in DIL_BRANCHES[1:]:
            orders_shape.append(jax.ShapeDtypeStruct((B, rr, S // rr, B_WIDTH), BF16))
            orders_spec.append(pl.BlockSpec((None, rr, TM_IN // rr, B_WIDTH), phase))
    out_shape = (
        jax.ShapeDtypeStruct((B, A_WIDTH, S), BF16),
        jax.ShapeDtypeStruct((B, S, 2 * A_KV_WIDTH), BF16),
        jax.ShapeDtypeStruct((B, A_KV_WIDTH, S), BF16),
        *orders_shape,
    )
    return pl.pallas_call(
        _inproj_kernel,
        grid=(nt, B),
        in_specs=[
            pl.BlockSpec((None, TM_IN, D), tok),
            pl.BlockSpec((1, D), const),
            pl.BlockSpec((D, in_width), const),
            pl.BlockSpec((1, LANES), const),
            pl.BlockSpec((1, LANES), const),
            pl.BlockSpec((TM_IN, LANES), lambda t, b: (t, 0)),
            pl.BlockSpec((TM_IN, LANES), lambda t, b: (t, 0)),
            pl.BlockSpec((2 * LANES, LANES), const),
        ],
        out_specs=(
            pl.BlockSpec((None, A_WIDTH, TM_IN), lambda t, b: (b, 0, t)),
            pl.BlockSpec((None, TM_IN, 2 * A_KV_WIDTH), tok),
            pl.BlockSpec((None, A_KV_WIDTH, TM_IN), lambda t, b: (b, 0, t)),
            *orders_spec,
        ),
        out_shape=out_shape,
        scratch_shapes=[pltpu.VMEM((B_WIDTH // LANES, TM_IN, LANES), F32)] * 3,
        compiler_params=pltpu.CompilerParams(
            dimension_semantics=("arbitrary", "arbitrary"), vmem_limit_bytes=VMEM_LIMIT),
        name="in_projection",
    )(x, g, w_bf16, gq, gk, cos, sin, ones2)


def _attn_a_kernel(qt_ref, k_ref, vt_ref, g_ref, o_ref):
    group = A_Q_HEADS // A_KV_HEADS
    S = k_ref.shape[0]
    top = lax.broadcasted_iota(jnp.int32, (LANES, TQ_A), 0) < HEAD_DIM
    ones = jnp.ones((16, S), BF16)
    logits = []
    for kv in range(A_KV_HEADS):
        cols = []
        for j in range(group // 2):
            r0 = (kv * group + 2 * j) * HEAD_DIM
            qt2 = qt_ref[r0:r0 + LANES, :]
            zero = jnp.zeros_like(qt2)
            cols += [jnp.where(top, qt2, zero), jnp.where(top, zero, qt2)]
        w = jnp.concatenate(cols, axis=1)
        k2 = k_ref[:, kv * LANES:(kv + 1) * LANES]
        logits.append(jnp.dot(k2, w, preferred_element_type=F32))
    blocks = []
    for kv in range(A_KV_HEADS):
        st = logits[kv]
        p = jnp.exp2(st - jnp.max(st, axis=0, keepdims=True)).astype(BF16)
        vaug = jnp.concatenate([vt_ref[kv * HEAD_DIM:(kv + 1) * HEAD_DIM, :], ones], axis=0)
        oa = jnp.dot(vaug, p, preferred_element_type=F32)
        ot = oa[0:HEAD_DIM] * (1.0 / oa[HEAD_DIM:HEAD_DIM + 1])
        for j in range(group // 2):
            pair_t = jnp.concatenate([ot[:, 2 * j * TQ_A:(2 * j + 1) * TQ_A],
                                      ot[:, (2 * j + 1) * TQ_A:(2 * j + 2) * TQ_A]], axis=0)
            blocks.append(pair_t.T)
    y = jnp.concatenate(blocks, axis=1)
    o_ref[...] = _rms(y, g_ref[...]).astype(BF16)


def _attention_a(qat, ka, vat, g):
    B, _, S = qat.shape
    return pl.pallas_call(
        _attn_a_kernel,
        grid=(B, S // TQ_A),
        in_specs=[
            pl.BlockSpec((None, A_WIDTH, TQ_A), lambda b, i: (b, 0, i)),
            pl.BlockSpec((None, S, 2 * A_KV_WIDTH), lambda b, i: (b, 0, 0)),
            pl.BlockSpec((None, A_KV_WIDTH, S), lambda b, i: (b, 0, 0)),
            pl.BlockSpec((1, A_WIDTH), lambda b, i: (0, 0)),
        ],
        out_specs=pl.BlockSpec((None, TQ_A, A_WIDTH), lambda b, i: (b, i, 0)),
        out_shape=jax.ShapeDtypeStruct((B, S, A_WIDTH), BF16),
        compiler_params=pltpu.CompilerParams(
            dimension_semantics=("arbitrary", "arbitrary"), vmem_limit_bytes=VMEM_LIMIT),
        name="attention_a",
    )(qat, ka, vat, g)


def _t5_bucket(rel):
    nb_half = N_BUCKETS // 2
    max_exact = nb_half // 2
    ret = jnp.where(rel > 0, nb_half, 0)
    n = jnp.abs(rel)
    large = max_exact + (jnp.log(jnp.maximum(n, 1).astype(F32) / max_exact)
                         / math.log(BUCKET_MAX_DIST / max_exact)
                         * (nb_half - max_exact)).astype(jnp.int32)
    large = jnp.minimum(large, nb_half - 1)
    return ret + jnp.where(n < max_exact, n, large)


def _variant_offset(v):
    return (0, -HALF_WINDOW, -2 * HALF_WINDOW)[v]


def _bias_kernel(rbt_ref, code_ref, valid_ref, o_ref):
    h = pl.program_id(0)
    rb = rbt_ref[pl.ds(h, 1), :]
    for n in range(len(DIL_BRANCHES)):
        code = code_ref[n:n + 1, :]
        acc = jnp.zeros(code.shape, F32)
        for b in range(N_BUCKETS):
            acc = jnp.where(code == b, rb[:, b:b + 1], acc)
        f = jnp.where(valid_ref[n:n + 1, :] > 0, acc * LOG2E, NEG)
        rows = jnp.broadcast_to(f, (TQ, BIAS_SPAN))
        for v in range(N_VARIANTS):
            shift = (KEY_WINDOW - _variant_offset(v)) % BIAS_SPAN
            o_ref[n * N_VARIANTS + v] = pltpu.roll(rows, shift, 1, stride=1, stride_axis=0)[:, 0:KEY_WINDOW]


def _bias_blocks(rel_bias):
    rel = jnp.arange(BIAS_SPAN, dtype=jnp.int32) - KEY_WINDOW
    code = jnp.stack([_t5_bucket(rel * r) for _, r in DIL_BRANCHES])
    valid = jnp.broadcast_to((jnp.abs(rel) <= HALF_WINDOW).astype(jnp.int32), code.shape)
    n_blocks = len(DIL_BRANCHES) * N_VARIANTS
    const = lambda h: (0, 0)
    return pl.pallas_call(
        _bias_kernel,
        grid=(B_HEADS,),
        in_specs=[pl.BlockSpec((B_HEADS, N_BUCKETS), const),
                  pl.BlockSpec(code.shape, const),
                  pl.BlockSpec(code.shape, const)],
        out_specs=pl.BlockSpec((n_blocks, None, TQ, KEY_WINDOW), lambda h: (0, h, 0, 0)),
        out_shape=jax.ShapeDtypeStruct((n_blocks, B_HEADS, TQ, KEY_WINDOW), F32),
        compiler_params=pltpu.CompilerParams(dimension_semantics=("arbitrary",)),
        name="bias_blocks",
    )(rel_bias.T, code, valid)


def _split_pair(q2):
    lane = lax.broadcasted_iota(jnp.int32, q2.shape, 1)
    low = lane < HEAD_DIM
    zero = jnp.zeros_like(q2)
    return jnp.concatenate([jnp.where(low, q2, zero), jnp.where(low, zero, q2)], axis=0)


def _attn_b_kernel(q1_ref, q4_ref, q16_ref, k1_ref, k4_ref, k16_ref, v1_ref, v4_ref, v16_ref,
                   bias_ref, g_ref, o_ref, y_ref, *acc_refs):
    pair = pl.program_id(1)
    S = q1_ref.shape[0]
    n_blocks = S // TQ
    low = lax.broadcasted_iota(jnp.int32, (TQ, LANES), 1) < HEAD_DIM
    q_refs = (q1_ref, q4_ref, q16_ref)
    k_refs = (k1_ref, k4_ref, k16_ref)
    v_refs = (v1_ref, v4_ref, v16_ref)

    def merge(x2):
        return jnp.where(low, x2[0:TQ], x2[TQ:2 * TQ])

    def banded_block(q2, k2, v2, bias):
        s = lax.dot_general(_split_pair(q2), k2, (((1,), (1,)), ((), ())),
                            preferred_element_type=F32) + bias
        m = jnp.max(s, axis=1, keepdims=True)
        p = jnp.exp2(s - m).astype(BF16)
        vaug = jnp.concatenate([v2, jnp.ones_like(v2)], axis=1)
        oa = jnp.dot(p, vaug, preferred_element_type=F32)
        den = merge(oa[:, LANES:2 * LANES])
        out = merge(oa[:, 0:LANES]) * (1.0 / den)
        lse = merge(jnp.broadcast_to(m, (2 * TQ, LANES))) + jnp.log2(den)
        return out, lse

    def step(i, carry):
        for n, (_, r) in enumerate(DIL_BRANCHES):
            length = S // r
            per_phase = length // TQ
            if per_phase > 1:
                ph = i // per_phase
                blk = i - ph * per_phase
                variant = (blk > 0).astype(jnp.int32) + (blk == per_phase - 1).astype(jnp.int32)
                r0 = pl.multiple_of(blk * TQ, TQ)
                w0 = pl.multiple_of(jnp.clip(blk * TQ - HALF_WINDOW, 0, length - KEY_WINDOW), HALF_WINDOW)
                width = KEY_WINDOW
            else:
                ph, blk, variant, r0, w0, width = i, 0, 0, 0, 0, length
            tab = n * N_VARIANTS + variant
            bias = jnp.concatenate([bias_ref[tab, 2 * pair, :, 0:width],
                                    bias_ref[tab, 2 * pair + 1, :, 0:width]], axis=0)
            if r == 1:
                q2 = q_refs[n][pl.ds(r0, TQ), :]
                k2 = k_refs[n][pl.ds(w0, width), :]
                v2 = v_refs[n][pl.ds(w0, width), :]
            else:
                q2 = q_refs[n][ph, pl.ds(r0, TQ), :]
                k2 = k_refs[n][ph, pl.ds(w0, width), :]
                v2 = v_refs[n][ph, pl.ds(w0, width), :]
            out, lse = banded_block(q2, k2, v2, bias)
            rows = pl.ds(r0 * r + ph, TQ, stride=r) if r > 1 else pl.ds(r0, TQ)
            acc_refs[2 * n][rows, :] = out
            acc_refs[2 * n + 1][rows, :] = lse
        return carry

    lax.fori_loop(0, n_blocks, step, 0, unroll=B_UNROLL)

    lses = [acc_refs[2 * n + 1][...] for n in range(len(DIL_BRANCHES))]
    top = functools.reduce(jnp.maximum, lses)
    weights = [jnp.exp2(l - top) for l in lses]
    num = sum(w * acc_refs[2 * n][...] for n, w in enumerate(weights))
    y = num * (1.0 / sum(weights))
    for c in range(B_HEADS // 2):
        @pl.when(pair == c)
        def _(c=c):
            y_ref[:, c * LANES:(c + 1) * LANES] = y

    @pl.when(pair == B_HEADS // 2 - 1)
    def _():
        o_ref[...] = _rms(y_ref[...], g_ref[...]).astype(BF16)


def _attention_b(orders, bias, g):
    B, S, _ = orders[0].shape
    specs = []
    for a in orders:
        if a.ndim == 3:
            specs.append(pl.BlockSpec((None, S, LANES), lambda b, j: (b, 0, j)))
        else:
            specs.append(pl.BlockSpec((None, a.shape[1], a.shape[2], LANES), lambda b, j: (b, 0, 0, j)))
    return pl.pallas_call(
        _attn_b_kernel,
        grid=(B, B_HEADS // 2),
        in_specs=specs + [
            pl.BlockSpec(bias.shape, lambda b, j: (0, 0, 0, 0), pipeline_mode=pl.Buffered(1)),
            pl.BlockSpec((1, B_WIDTH), lambda b, j: (0, 0)),
        ],
        out_specs=pl.BlockSpec((None, S, B_WIDTH), lambda b, j: (b, 0, 0)),
        out_shape=jax.ShapeDtypeStruct((B, S, B_WIDTH), BF16),
        scratch_shapes=[pltpu.VMEM((S, B_WIDTH), F32)] + [pltpu.VMEM((S, LANES), F32)] * 6,
        compiler_params=pltpu.CompilerParams(
            dimension_semantics=("arbitrary", "arbitrary"), vmem_limit_bytes=VMEM_LIMIT),
        name="attention_b",
    )(*orders, bias, g)


def _out_mlp_kernel(x_ref, ma_ref, mb_ref, wo_ref, gm_ref, wup_ref, wdn_ref, gf_ref, o_ref):
    mix = jnp.concatenate([ma_ref[...], mb_ref[...]], axis=1)
    h = x_ref[...] + jnp.dot(mix, wo_ref[...], preferred_element_type=F32)
    hn = _rms(h, gm_ref[...]).astype(BF16)
    acc = h
    d_ff = wup_ref.shape[1]
    for c in range(d_ff // FF_CHUNK):
        sl = slice(c * FF_CHUNK, (c + 1) * FF_CHUNK)
        u = jnp.maximum(jnp.dot(hn, wup_ref[:, sl], preferred_element_type=F32), 0.0)
        acc = acc + jnp.dot((u * u).astype(BF16), wdn_ref[sl, :], preferred_element_type=F32)
    o_ref[...] = _rms(acc, gf_ref[...])


def _out_mlp(x2, ma, mb, wo, gm, wup, wdn, gf):
    N, D = x2.shape
    d_ff = wup.shape[1]
    tok = lambda t: (t, 0)
    const = lambda t: (0, 0)
    resident = functools.partial(pl.BlockSpec, index_map=const, pipeline_mode=pl.Buffered(1))
    return pl.pallas_call(
        _out_mlp_kernel,
        grid=(N // TM_OUT,),
        in_specs=[
            pl.BlockSpec((TM_OUT, D), tok),
            pl.BlockSpec((TM_OUT, A_WIDTH), tok),
            pl.BlockSpec((TM_OUT, B_WIDTH), tok),
            resident((A_WIDTH + B_WIDTH, D)),
            pl.BlockSpec((1, D), const),
            resident((D, d_ff)),
            resident((d_ff, D)),
            pl.BlockSpec((1, D), const),
        ],
        out_specs=pl.BlockSpec((TM_OUT, D), tok),
        out_shape=jax.ShapeDtypeStruct((N, D), F32),
        compiler_params=pltpu.CompilerParams(
            dimension_semantics=("arbitrary",), vmem_limit_bytes=VMEM_LIMIT),
        name="out_mlp",
    )(x2, ma, mb, wo, gm, wup, wdn, gf)


def kernel(x, attn_norm_g, w_in, q_norm_g, k_norm_g, rel_bias, out_norm_a_g, out_norm_b_g,
           w_out, mlp_norm_g, w_up, w_down, final_norm_g):
    B, S, D = x.shape
    assert w_in.shape[0] == 1, "the output kernel fuses the final norm: single layer only"
    bias = _bias_blocks(rel_bias)
    gq = jnp.tile(q_norm_g[0], LANES // HEAD_DIM).reshape(1, LANES)
    gk = jnp.tile(k_norm_g[0], LANES // HEAD_DIM).reshape(1, LANES)
    qat, ka, vat, *orders = _in_projection(
        x, attn_norm_g[0].reshape(1, D), w_in[0].astype(BF16), gq, gk)
    ma = _attention_a(qat, ka, vat, out_norm_a_g[0].reshape(1, A_WIDTH))
    mb = _attention_b(orders, bias, out_norm_b_g[0].reshape(1, B_WIDTH))
    out = _out_mlp(x.reshape(B * S, D), ma.reshape(B * S, A_WIDTH), mb.reshape(B * S, B_WIDTH),
                   w_out[0].astype(BF16), mlp_norm_g[0].reshape(1, D),
                   w_up[0].astype(BF16), w_down[0].astype(BF16), final_norm_g.reshape(1, D))
    return out.reshape(B, S, D)
```

```python
import functools
import math

import jax
import jax.numpy as jnp
import numpy as np
from jax import lax
from jax.experimental import pallas as pl
from jax.experimental.pallas import tpu as pltpu

HEAD_DIM = 64
A_Q_HEADS = 8
A_KV_HEADS = 2
B_HEADS = 8
GRID_W = 64
ROPE_THETA = 10000.0
DIL_BRANCHES = ((128, 1), (512, 4), (2048, 16))
N_BUCKETS = 32
BUCKET_MAX_DIST = 1024
EPS = 1e-6
NEG = -1e30

A_WIDTH = A_Q_HEADS * HEAD_DIM
A_KV_WIDTH = A_KV_HEADS * HEAD_DIM
B_WIDTH = B_HEADS * HEAD_DIM
LOG2E = math.log2(math.e)
Q_SCALE = HEAD_DIM ** -0.5 * LOG2E

LANES = 128
TM_IN = 512
TQ = 128
TQ_A = 512
TM_OUT = 1024
FF_CHUNK = 1024
VMEM_LIMIT = 56 * 1024 * 1024

HALF_WINDOW = DIL_BRANCHES[0][0] // (2 * DIL_BRANCHES[0][1])
assert all(w // (2 * r) == HALF_WINDOW for w, r in DIL_BRANCHES)
KEY_WINDOW = TQ + 2 * HALF_WINDOW
BIAS_SPAN = 2 * KEY_WINDOW
N_VARIANTS = 3
B_UNROLL = 2

F32 = jnp.float32
BF16 = jnp.bfloat16


def _rms(x, g):
    return x * lax.rsqrt(jnp.mean(x * x, axis=-1, keepdims=True) + EPS) * g


def _inproj_kernel(x_ref, g_ref, w_ref, gq_ref, gk_ref, cos_ref, sin_ref, ones_ref,
                   qat_ref, ka_ref, vat_ref,
                   q1_ref, q4_ref, q16_ref, k1_ref, k4_ref, k16_ref, v1_ref, v4_ref, v16_ref,
                   sq_ref, sk_ref, sv_ref):
    n = _rms(x_ref[...], g_ref[...]).astype(BF16)
    cos = cos_ref[...]
    sin = sin_ref[...]
    ones2 = ones_ref[...]
    lane = lax.broadcasted_iota(jnp.int32, (TM_IN, LANES), 1)
    first_half = (lane % 32) < 16

    def head_norm_rope(blk, gain):
        sq = blk * blk
        hi = sq.astype(BF16)
        lo = (sq - hi.astype(F32)).astype(BF16)
        ss = jnp.dot(jnp.concatenate([hi, lo], axis=1), ones2, preferred_element_type=F32)
        xn = blk * lax.rsqrt(ss * (1.0 / HEAD_DIM) + EPS) * gain
        partner = jnp.where(first_half, pltpu.roll(xn, LANES - 16, 1), pltpu.roll(xn, 16, 1))
        return xn * cos + partner * sin

    o_ka = A_WIDTH
    o_va = o_ka + A_KV_WIDTH
    o_qb = o_va + A_KV_WIDTH
    o_kb = o_qb + B_WIDTH
    o_vb = o_kb + B_WIDTH

    def stash(val, scr_ref, nat_ref):
        nat_ref[...] = val.astype(BF16)
        for c in range(B_WIDTH // LANES):
            scr_ref[c] = val[:, c * LANES:(c + 1) * LANES]

    def write_phases(scr_ref, ph_refs):
        for c in range(B_WIDTH // LANES):
            sl = slice(c * LANES, (c + 1) * LANES)
            for (_, r), ph_ref in zip(DIL_BRANCHES[1:], ph_refs):
                for p in range(r):
                    ph_ref[p, :, sl] = scr_ref[c, pl.ds(p, TM_IN // r, stride=r), :].astype(BF16)

    pq = jnp.dot(n, w_ref[:, 0:o_ka], preferred_element_type=F32)
    stash(jnp.dot(n, w_ref[:, o_qb:o_kb], preferred_element_type=F32) * Q_SCALE, sq_ref, q1_ref)
    gq = gq_ref[...]
    for j in range(A_WIDTH // LANES):
        sl = slice(j * LANES, (j + 1) * LANES)
        qat_ref[sl, :] = (head_norm_rope(pq[:, sl], gq) * Q_SCALE).T.astype(BF16)

    pk = jnp.dot(n, w_ref[:, o_ka:o_va], preferred_element_type=F32)
    pv = jnp.dot(n, w_ref[:, o_va:o_qb], preferred_element_type=F32)
    stash(jnp.dot(n, w_ref[:, o_kb:o_vb], preferred_element_type=F32), sk_ref, k1_ref)
    kr = head_norm_rope(pk, gk_ref[...])
    swapped = pltpu.roll(kr, HEAD_DIM, 1)
    low = lane < HEAD_DIM
    ka_ref[:, 0:LANES] = jnp.where(low, kr, swapped).astype(BF16)
    ka_ref[:, LANES:2 * LANES] = jnp.where(low, swapped, kr).astype(BF16)
    vat_ref[...] = pv.T.astype(BF16)
    write_phases(sq_ref, (q4_ref, q16_ref))

    stash(jnp.dot(n, w_ref[:, o_vb:], preferred_element_type=F32), sv_ref, v1_ref)
    write_phases(sk_ref, (k4_ref, k16_ref))
    write_phases(sv_ref, (v4_ref, v16_ref))


def _rope_tables(S):
    rows = S // GRID_W
    row = jnp.repeat(jnp.arange(rows, dtype=F32), GRID_W)
    col = jnp.tile(jnp.arange(GRID_W, dtype=F32), rows)
    n_freq = HEAD_DIM // 4
    inv = ROPE_THETA ** (-jnp.arange(n_freq, dtype=F32) / n_freq)
    ang_r = row[:, None] * inv[None, :]
    ang_c = col[:, None] * inv[None, :]
    cos_h = jnp.concatenate([jnp.cos(ang_r)] * 2 + [jnp.cos(ang_c)] * 2, axis=1)
    sin_h = jnp.concatenate([-jnp.sin(ang_r), jnp.sin(ang_r), -jnp.sin(ang_c), jnp.sin(ang_c)], axis=1)
    return jnp.tile(cos_h, (1, 2)), jnp.tile(sin_h, (1, 2))


def _in_projection(x, g, w_bf16, gq, gk):
    B, S, D = x.shape
    in_width = w_bf16.shape[1]
    cos, sin = _rope_tables(S)
    r = np.arange(2 * LANES)[:, None]
    c = np.arange(LANES)[None, :]
    ones2 = jnp.asarray(((r % LANES) // HEAD_DIM) == (c // HEAD_DIM), dtype=BF16)
    nt = S // TM_IN
    tok = lambda t, b: (b, t, 0)
    phase = lambda t, b: (b, 0, t, 0)
    const = lambda t, b: (0, 0)
    orders_shape, orders_spec = [], []
    for _ in range(3):
        orders_shape.append(jax.ShapeDtypeStruct((B, S, B_WIDTH), BF16))
        orders_spec.append(pl.BlockSpec((None, TM_IN, B_WIDTH), tok))
        for _, rr in DIL_BRANCHES[1:]:
            orders_shape.append(jax.ShapeDtypeStruct((B, rr, S // rr, B_WIDTH), BF16))
            orders_spec.append(pl.BlockSpec((None, rr, TM_IN // rr, B_WIDTH), phase))
    out_shape = (
        jax.ShapeDtypeStruct((B, A_WIDTH, S), BF16),
        jax.ShapeDtypeStruct((B, S, 2 * A_KV_WIDTH), BF16),
        jax.ShapeDtypeStruct((B, A_KV_WIDTH, S), BF16),
        *orders_shape,
    )
    return pl.pallas_call(
        _inproj_kernel,
        grid=(nt, B),
        in_specs=[
            pl.BlockSpec((None, TM_IN, D), tok),
            pl.BlockSpec((1, D), const),
            pl.BlockSpec((D, in_width), const),
            pl.BlockSpec((1, LANES), const),
            pl.BlockSpec((1, LANES), const),
            pl.BlockSpec((TM_IN, LANES), lambda t, b: (t, 0)),
            pl.BlockSpec((TM_IN, LANES), lambda t, b: (t, 0)),
            pl.BlockSpec((2 * LANES, LANES), const),
        ],
        out_specs=(
            pl.BlockSpec((None, A_WIDTH, TM_IN), lambda t, b: (b, 0, t)),
            pl.BlockSpec((None, TM_IN, 2 * A_KV_WIDTH), tok),
            pl.BlockSpec((None, A_KV_WIDTH, TM_IN), lambda t, b: (b, 0, t)),
            *orders_spec,
        ),
        out_shape=out_shape,
        scratch_shapes=[pltpu.VMEM((B_WIDTH // LANES, TM_IN, LANES), F32)] * 3,
        compiler_params=pltpu.CompilerParams(
            dimension_semantics=("arbitrary", "arbitrary"), vmem_limit_bytes=VMEM_LIMIT),
        name="in_projection",
    )(x, g, w_bf16, gq, gk, cos, sin, ones2)


def _attn_a_kernel(qt_ref, k_ref, vt_ref, g_ref, o_ref):
    group = A_Q_HEADS // A_KV_HEADS
    S = k_ref.shape[0]
    top = lax.broadcasted_iota(jnp.int32, (LANES, TQ_A), 0) < HEAD_DIM
    ones = jnp.ones((16, S), BF16)
    logits = []
    for kv in range(A_KV_HEADS):
        cols = []
        for j in range(group // 2):
            r0 = (kv * group + 2 * j) * HEAD_DIM
            qt2 = qt_ref[r0:r0 + LANES, :]
            zero = jnp.zeros_like(qt2)
            cols += [jnp.where(top, qt2, zero), jnp.where(top, zero, qt2)]
        w = jnp.concatenate(cols, axis=1)
        k2 = k_ref[:, kv * LANES:(kv + 1) * LANES]
        logits.append(jnp.dot(k2, w, preferred_element_type=F32))
    blocks = []
    for kv in range(A_KV_HEADS):
        st = logits[kv]
        p = jnp.exp2(st - jnp.max(st, axis=0, keepdims=True)).astype(BF16)
        vaug = jnp.concatenate([vt_ref[kv * HEAD_DIM:(kv + 1) * HEAD_DIM, :], ones], axis=0)
        oa = jnp.dot(vaug, p, preferred_element_type=F32)
        ot = oa[0:HEAD_DIM] * (1.0 / oa[HEAD_DIM:HEAD_DIM + 1])
        for j in range(group // 2):
            pair_t = jnp.concatenate([ot[:, 2 * j * TQ_A:(2 * j + 1) * TQ_A],
                                      ot[:, (2 * j + 1) * TQ_A:(2 * j + 2) * TQ_A]], axis=0)
            blocks.append(pair_t.T)
    y = jnp.concatenate(blocks, axis=1)
    o_ref[...] = _rms(y, g_ref[...]).astype(BF16)


def _attention_a(qat, ka, vat, g):
    B, _, S = qat.shape
    return pl.pallas_call(
        _attn_a_kernel,
        grid=(B, S // TQ_A),
        in_specs=[
            pl.BlockSpec((None, A_WIDTH, TQ_A), lambda b, i: (b, 0, i)),
            pl.BlockSpec((None, S, 2 * A_KV_WIDTH), lambda b, i: (b, 0, 0)),
            pl.BlockSpec((None, A_KV_WIDTH, S), lambda b, i: (b, 0, 0)),
            pl.BlockSpec((1, A_WIDTH), lambda b, i: (0, 0)),
        ],
        out_specs=pl.BlockSpec((None, TQ_A, A_WIDTH), lambda b, i: (b, i, 0)),
        out_shape=jax.ShapeDtypeStruct((B, S, A_WIDTH), BF16),
        compiler_params=pltpu.CompilerParams(
            dimension_semantics=("arbitrary", "arbitrary"), vmem_limit_bytes=VMEM_LIMIT),
        name="attention_a",
    )(qat, ka, vat, g)


def _t5_bucket(rel):
    nb_half = N_BUCKETS // 2
    max_exact = nb_half // 2
    ret = jnp.where(rel > 0, nb_half, 0)
    n = jnp.abs(rel)
    large = max_exact + (jnp.log(jnp.maximum(n, 1).astype(F32) / max_exact)
                         / math.log(BUCKET_MAX_DIST / max_exact)
                         * (nb_half - max_exact)).astype(jnp.int32)
    large = jnp.minimum(large, nb_half - 1)
    return ret + jnp.where(n < max_exact, n, large)


def _variant_offset(v):
    return (0, -HALF_WINDOW, -2 * HALF_WINDOW)[v]


def _bias_kernel(rbt_ref, code_ref, valid_ref, o_ref):
    h = pl.program_id(0)
    rb = rbt_ref[pl.ds(h, 1), :]
    for n in range(len(DIL_BRANCHES)):
        code = code_ref[n:n + 1, :]
        acc = jnp.zeros(code.shape, F32)
        for b in range(N_BUCKETS):
            acc = jnp.where(code == b, rb[:, b:b + 1], acc)
        f = jnp.where(valid_ref[n:n + 1, :] > 0, acc * LOG2E, NEG)
        rows = jnp.broadcast_to(f, (TQ, BIAS_SPAN))
        for v in range(N_VARIANTS):
            shift = (KEY_WINDOW - _variant_offset(v)) % BIAS_SPAN
            o_ref[n * N_VARIANTS + v] = pltpu.roll(rows, shift, 1, stride=1, stride_axis=0)[:, 0:KEY_WINDOW]


def _bias_blocks(rel_bias):
    rel = jnp.arange(BIAS_SPAN, dtype=jnp.int32) - KEY_WINDOW
    code = jnp.stack([_t5_bucket(rel * r) for _, r in DIL_BRANCHES])
    valid = jnp.broadcast_to((jnp.abs(rel) <= HALF_WINDOW).astype(jnp.int32), code.shape)
    n_blocks = len(DIL_BRANCHES) * N_VARIANTS
    const = lambda h: (0, 0)
    return pl.pallas_call(
        _bias_kernel,
        grid=(B_HEADS,),
        in_specs=[pl.BlockSpec((B_HEADS, N_BUCKETS), const),
                  pl.BlockSpec(code.shape, const),
                  pl.BlockSpec(code.shape, const)],
        out_specs=pl.BlockSpec((n_blocks, None, TQ, KEY_WINDOW), lambda h: (0, h, 0, 0)),
        out_shape=jax.ShapeDtypeStruct((n_blocks, B_HEADS, TQ, KEY_WINDOW), F32),
        compiler_params=pltpu.CompilerParams(dimension_semantics=("arbitrary",)),
        name="bias_blocks",
    )(rel_bias.T, code, valid)


def _split_pair(q2):
    lane = lax.broadcasted_iota(jnp.int32, q2.shape, 1)
    low = lane < HEAD_DIM
    zero = jnp.zeros_like(q2)
    return jnp.concatenate([jnp.where(low, q2, zero), jnp.where(low, zero, q2)], axis=0)


def _attn_b_kernel(q1_ref, q4_ref, q16_ref, k1_ref, k4_ref, k16_ref, v1_ref, v4_ref, v16_ref,
                   bias_ref, g_ref, o_ref, y_ref, *acc_refs):
    pair = pl.program_id(1)
    S = q1_ref.shape[0]
    n_blocks = S // TQ
    low = lax.broadcasted_iota(jnp.int32, (TQ, LANES), 1) < HEAD_DIM
    q_refs = (q1_ref, q4_ref, q16_ref)
    k_refs = (k1_ref, k4_ref, k16_ref)
    v_refs = (v1_ref, v4_ref, v16_ref)

    def merge(x2):
        return jnp.where(low, x2[0:TQ], x2[TQ:2 * TQ])

    def banded_block(q2, k2, v2, bias):
        s = lax.dot_general(_split_pair(q2), k2, (((1,), (1,)), ((), ())),
                            preferred_element_type=F32) + bias
        m = jnp.max(s, axis=1, keepdims=True)
        p = jnp.exp2(s - m).astype(BF16)
        vaug = jnp.concatenate([v2, jnp.ones_like(v2)], axis=1)
        oa = jnp.dot(p, vaug, preferred_element_type=F32)
        den = merge(oa[:, LANES:2 * LANES])
        out = merge(oa[:, 0:LANES]) * (1.0 / den)
        lse = merge(jnp.broadcast_to(m, (2 * TQ, LANES))) + jnp.log2(den)
        return out, lse

    def step(i, carry):
        for n, (_, r) in enumerate(DIL_BRANCHES):
            length = S // r
            per_phase = length // TQ
            if per_phase > 1:
                ph = i // per_phase
                blk = i - ph * per_phase
                variant = jnp.where(blk == 0, 0, jnp.where(blk == per_phase - 1, 2, 1))
                r0 = pl.multiple_of(blk * TQ, TQ)
                w0 = pl.multiple_of(jnp.clip(blk * TQ - HALF_WINDOW, 0, length - KEY_WINDOW), HALF_WINDOW)
                width = KEY_WINDOW
            else:
                ph, blk, variant, r0, w0, width = i, 0, 0, 0, 0, length
            tab = n * N_VARIANTS + variant
            bias = jnp.concatenate([bias_ref[tab, 2 * pair, :, 0:width],
                                    bias_ref[tab, 2 * pair + 1, :, 0:width]], axis=0)
            if r == 1:
                q2 = q_refs[n][pl.ds(r0, TQ), :]
                k2 = k_refs[n][pl.ds(w0, width), :]
                v2 = v_refs[n][pl.ds(w0, width), :]
            else:
                q2 = q_refs[n][ph, pl.ds(r0, TQ), :]
                k2 = k_refs[n][ph, pl.ds(w0, width), :]
                v2 = v_refs[n][ph, pl.ds(w0, width), :]
            out, lse = banded_block(q2, k2, v2, bias)
            rows = pl.ds(r0 * r + ph, TQ, stride=r) if r > 1 else pl.ds(r0, TQ)
            acc_refs[2 * n][rows, :] = out
            acc_refs[2 * n + 1][rows, :] = lse
        return carry

    lax.fori_loop(0, n_blocks, step, 0, unroll=B_UNROLL)

    lses = [acc_refs[2 * n + 1][...] for n in range(len(DIL_BRANCHES))]
    top = functools.reduce(jnp.maximum, lses)
    weights = [jnp.exp2(l - top) for l in lses]
    num = sum(w * acc_refs[2 * n][...] for n, w in enumerate(weights))
    y = num * (1.0 / sum(weights))
    for c in range(B_HEADS // 2):
        @pl.when(pair == c)
        def _(c=c):
            y_ref[:, c * LANES:(c + 1) * LANES] = y

    @pl.when(pair == B_HEADS // 2 - 1)
    def _():
        o_ref[...] = _rms(y_ref[...], g_ref[...]).astype(BF16)


def _attention_b(orders, bias, g):
    B, S, _ = orders[0].shape
    specs = []
    for a in orders:
        if a.ndim == 3:
            specs.append(pl.BlockSpec((None, S, LANES), lambda b, j: (b, 0, j)))
        else:
            specs.append(pl.BlockSpec((None, a.shape[1], a.shape[2], LANES), lambda b, j: (b, 0, 0, j)))
    return pl.pallas_call(
        _attn_b_kernel,
        grid=(B, B_HEADS // 2),
        in_specs=specs + [
            pl.BlockSpec(bias.shape, lambda b, j: (0, 0, 0, 0), pipeline_mode=pl.Buffered(1)),
            pl.BlockSpec((1, B_WIDTH), lambda b, j: (0, 0)),
        ],
        out_specs=pl.BlockSpec((None, S, B_WIDTH), lambda b, j: (b, 0, 0)),
        out_shape=jax.ShapeDtypeStruct((B, S, B_WIDTH), BF16),
        scratch_shapes=[pltpu.VMEM((S, B_WIDTH), F32)] + [pltpu.VMEM((S, LANES), F32)] * 6,
        compiler_params=pltpu.CompilerParams(
            dimension_semantics=("arbitrary", "arbitrary"), vmem_limit_bytes=VMEM_LIMIT),
        name="attention_b",
    )(*orders, bias, g)


def _out_mlp_kernel(x_ref, ma_ref, mb_ref, wo_ref, gm_ref, wup_ref, wdn_ref, gf_ref, o_ref):
    mix = jnp.concatenate([ma_ref[...], mb_ref[...]], axis=1)
    h = x_ref[...] + jnp.dot(mix, wo_ref[...], preferred_element_type=F32)
    hn = _rms(h, gm_ref[...]).astype(BF16)
    acc = h
    d_ff = wup_ref.shape[1]
    for c in range(d_ff // FF_CHUNK):
        sl = slice(c * FF_CHUNK, (c + 1) * FF_CHUNK)
        u = jnp.maximum(jnp.dot(hn, wup_ref[:, sl], preferred_element_type=F32), 0.0)
        acc = acc + jnp.dot((u * u).astype(BF16), wdn_ref[sl, :], preferred_element_type=F32)
    o_ref[...] = _rms(acc, gf_ref[...])


def _out_mlp(x2, ma, mb, wo, gm, wup, wdn, gf):
    N, D = x2.shape
    d_ff = wup.shape[1]
    tok = lambda t: (t, 0)
    const = lambda t: (0, 0)
    resident = functools.partial(pl.BlockSpec, index_map=const, pipeline_mode=pl.Buffered(1))
    return pl.pallas_call(
        _out_mlp_kernel,
        grid=(N // TM_OUT,),
        in_specs=[
            pl.BlockSpec((TM_OUT, D), tok),
            pl.BlockSpec((TM_OUT, A_WIDTH), tok),
            pl.BlockSpec((TM_OUT, B_WIDTH), tok),
            resident((A_WIDTH + B_WIDTH, D)),
            pl.BlockSpec((1, D), const),
            resident((D, d_ff)),
            resident((d_ff, D)),
            pl.BlockSpec((1, D), const),
        ],
        out_specs=pl.BlockSpec((TM_OUT, D), tok),
        out_shape=jax.ShapeDtypeStruct((N, D), F32),
        compiler_params=pltpu.CompilerParams(
            dimension_semantics=("arbitrary",), vmem_limit_bytes=VMEM_LIMIT),
        name="out_mlp",
    )(x2, ma, mb, wo, gm, wup, wdn, gf)


def kernel(x, attn_norm_g, w_in, q_norm_g, k_norm_g, rel_bias, out_norm_a_g, out_norm_b_g,
           w_out, mlp_norm_g, w_up, w_down, final_norm_g):
    B, S, D = x.shape
    assert w_in.shape[0] == 1, "the output kernel fuses the final norm: single layer only"
    bias = _bias_blocks(rel_bias)
    gq = jnp.tile(q_norm_g[0], LANES // HEAD_DIM).reshape(1, LANES)
    gk = jnp.tile(k_norm_g[0], LANES // HEAD_DIM).reshape(1, LANES)
    qat, ka, vat, *orders = _in_projection(
        x, attn_norm_g[0].reshape(1, D), w_in[0].astype(BF16), gq, gk)
    ma = _attention_a(qat, ka, vat, out_norm_a_g[0].reshape(1, A_WIDTH))
    mb = _attention_b(orders, bias, out_norm_b_g[0].reshape(1, B_WIDTH))
    out = _out_mlp(x.reshape(B * S, D), ma.reshape(B * S, A_WIDTH), mb.reshape(B * S, B_WIDTH),
                   w_out[0].astype(BF16), mlp_norm_g[0].reshape(1, D),
                   w_up[0].astype(BF16), w_down[0].astype(BF16), final_norm_g.reshape(1, D))
    return out.reshape(B, S, D)
```

```python
import functools
import math

import jax
import jax.numpy as jnp
import numpy as np
from jax import lax
from jax.experimental import pallas as pl
from jax.experimental.pallas import tpu as pltpu

HEAD_DIM = 64
A_Q_HEADS = 8
A_KV_HEADS = 2
B_HEADS = 8
GRID_W = 64
ROPE_THETA = 10000.0
DIL_BRANCHES = ((128, 1), (512, 4), (2048, 16))
N_BUCKETS = 32
BUCKET_MAX_DIST = 1024
EPS = 1e-6
NEG = -1e30

A_WIDTH = A_Q_HEADS * HEAD_DIM
A_KV_WIDTH = A_KV_HEADS * HEAD_DIM
B_WIDTH = B_HEADS * HEAD_DIM
LOG2E = math.log2(math.e)
Q_SCALE = HEAD_DIM ** -0.5 * LOG2E

LANES = 128
TM_IN = 512
TQ = 128
TQ_A = 256
KEY_CHUNK = 256
TM_OUT = 1024
FF_CHUNK = 1024
VMEM_LIMIT = 56 * 1024 * 1024

HALF_WINDOW = DIL_BRANCHES[0][0] // (2 * DIL_BRANCHES[0][1])
assert all(w // (2 * r) == HALF_WINDOW for w, r in DIL_BRANCHES)
KEY_WINDOW = TQ + 2 * HALF_WINDOW
BIAS_SPAN = 2 * KEY_WINDOW
N_VARIANTS = 3
B_UNROLL = 2

F32 = jnp.float32
BF16 = jnp.bfloat16


def _rms(x, g):
    return x * lax.rsqrt(jnp.mean(x * x, axis=-1, keepdims=True) + EPS) * g


def _inproj_kernel(x_ref, g_ref, w_ref, gq_ref, gk_ref, cos_ref, sin_ref, ones_ref,
                   qat_ref, ka_ref, vat_ref,
                   q1_ref, q4_ref, q16_ref, k1_ref, k4_ref, k16_ref, v1_ref, v4_ref, v16_ref,
                   sq_ref, sk_ref, sv_ref):
    n = _rms(x_ref[...], g_ref[...]).astype(BF16)
    cos = cos_ref[...]
    sin = sin_ref[...]
    ones2 = ones_ref[...]
    lane = lax.broadcasted_iota(jnp.int32, (TM_IN, LANES), 1)
    first_half = (lane % 32) < 16

    def head_norm_rope(blk, gain):
        sq = blk * blk
        hi = sq.astype(BF16)
        lo = (sq - hi.astype(F32)).astype(BF16)
        ss = jnp.dot(jnp.concatenate([hi, lo], axis=1), ones2, preferred_element_type=F32)
        xn = blk * lax.rsqrt(ss * (1.0 / HEAD_DIM) + EPS) * gain
        partner = jnp.where(first_half, pltpu.roll(xn, LANES - 16, 1), pltpu.roll(xn, 16, 1))
        return xn * cos + partner * sin

    o_ka = A_WIDTH
    o_va = o_ka + A_KV_WIDTH
    o_qb = o_va + A_KV_WIDTH
    o_kb = o_qb + B_WIDTH
    o_vb = o_kb + B_WIDTH

    def stash(val, scr_ref, nat_ref):
        nat_ref[...] = val.astype(BF16)
        for c in range(B_WIDTH // LANES):
            scr_ref[c] = val[:, c * LANES:(c + 1) * LANES]

    def write_phases(scr_ref, ph_refs):
        for c in range(B_WIDTH // LANES):
            sl = slice(c * LANES, (c + 1) * LANES)
            for (_, r), ph_ref in zip(DIL_BRANCHES[1:], ph_refs):
                for p in range(r):
                    ph_ref[p, :, sl] = scr_ref[c, pl.ds(p, TM_IN // r, stride=r), :].astype(BF16)

    pq = jnp.dot(n, w_ref[:, 0:o_ka], preferred_element_type=F32)
    stash(jnp.dot(n, w_ref[:, o_qb:o_kb], preferred_element_type=F32) * Q_SCALE, sq_ref, q1_ref)
    gq = gq_ref[...]
    for j in range(A_WIDTH // LANES):
        sl = slice(j * LANES, (j + 1) * LANES)
        qat_ref[sl, :] = (head_norm_rope(pq[:, sl], gq) * Q_SCALE).T.astype(BF16)

    pk = jnp.dot(n, w_ref[:, o_ka:o_va], preferred_element_type=F32)
    pv = jnp.dot(n, w_ref[:, o_va:o_qb], preferred_element_type=F32)
    stash(jnp.dot(n, w_ref[:, o_kb:o_vb], preferred_element_type=F32), sk_ref, k1_ref)
    kr = head_norm_rope(pk, gk_ref[...])
    swapped = pltpu.roll(kr, HEAD_DIM, 1)
    low = lane < HEAD_DIM
    ka_ref[:, 0:LANES] = jnp.where(low, kr, swapped).astype(BF16)
    ka_ref[:, LANES:2 * LANES] = jnp.where(low, swapped, kr).astype(BF16)
    vat_ref[...] = pv.T.astype(BF16)
    write_phases(sq_ref, (q4_ref, q16_ref))

    stash(jnp.dot(n, w_ref[:, o_vb:], preferred_element_type=F32), sv_ref, v1_ref)
    write_phases(sk_ref, (k4_ref, k16_ref))
    write_phases(sv_ref, (v4_ref, v16_ref))


def _rope_tables(S):
    rows = S // GRID_W
    row = jnp.repeat(jnp.arange(rows, dtype=F32), GRID_W)
    col = jnp.tile(jnp.arange(GRID_W, dtype=F32), rows)
    n_freq = HEAD_DIM // 4
    inv = ROPE_THETA ** (-jnp.arange(n_freq, dtype=F32) / n_freq)
    ang_r = row[:, None] * inv[None, :]
    ang_c = col[:, None] * inv[None, :]
    cos_h = jnp.concatenate([jnp.cos(ang_r)] * 2 + [jnp.cos(ang_c)] * 2, axis=1)
    sin_h = jnp.concatenate([-jnp.sin(ang_r), jnp.sin(ang_r), -jnp.sin(ang_c), jnp.sin(ang_c)], axis=1)
    return jnp.tile(cos_h, (1, 2)), jnp.tile(sin_h, (1, 2))


def _in_projection(x, g, w_bf16, gq, gk):
    B, S, D = x.shape
    in_width = w_bf16.shape[1]
    cos, sin = _rope_tables(S)
    r = np.arange(2 * LANES)[:, None]
    c = np.arange(LANES)[None, :]
    ones2 = jnp.asarray(((r % LANES) // HEAD_DIM) == (c // HEAD_DIM), dtype=BF16)
    nt = S // TM_IN
    tok = lambda t, b: (b, t, 0)
    phase = lambda t, b: (b, 0, t, 0)
    const = lambda t, b: (0, 0)
    orders_shape, orders_spec = [], []
    for _ in range(3):
        orders_shape.append(jax.ShapeDtypeStruct((B, S, B_WIDTH), BF16))
        orders_spec.append(pl.BlockSpec((None, TM_IN, B_WIDTH), tok))
        for _, rr in DIL_BRANCHES[1:]:
            orders_shape.append(jax.ShapeDtypeStruct((B, rr, S // rr, B_WIDTH), BF16))
            orders_spec.append(pl.BlockSpec((None, rr, TM_IN // rr, B_WIDTH), phase))
    out_shape = (
        jax.ShapeDtypeStruct((B, A_WIDTH, S), BF16),
        jax.ShapeDtypeStruct((B, S, 2 * A_KV_WIDTH), BF16),
        jax.ShapeDtypeStruct((B, A_KV_WIDTH, S), BF16),
        *orders_shape,
    )
    return pl.pallas_call(
        _inproj_kernel,
        grid=(nt, B),
        in_specs=[
            pl.BlockSpec((None, TM_IN, D), tok),
            pl.BlockSpec((1, D), const),
            pl.BlockSpec((D, in_width), const),
            pl.BlockSpec((1, LANES), const),
            pl.BlockSpec((1, LANES), const),
            pl.BlockSpec((TM_IN, LANES), lambda t, b: (t, 0)),
            pl.BlockSpec((TM_IN, LANES), lambda t, b: (t, 0)),
            pl.BlockSpec((2 * LANES, LANES), const),
        ],
        out_specs=(
            pl.BlockSpec((None, A_WIDTH, TM_IN), lambda t, b: (b, 0, t)),
            pl.BlockSpec((None, TM_IN, 2 * A_KV_WIDTH), tok),
            pl.BlockSpec((None, A_KV_WIDTH, TM_IN), lambda t, b: (b, 0, t)),
            *orders_spec,
        ),
        out_shape=out_shape,
        scratch_shapes=[pltpu.VMEM((B_WIDTH // LANES, TM_IN, LANES), F32)] * 3,
        compiler_params=pltpu.CompilerParams(
            dimension_semantics=("arbitrary", "arbitrary"), vmem_limit_bytes=VMEM_LIMIT),
        name="in_projection",
    )(x, g, w_bf16, gq, gk, cos, sin, ones2)


def _attn_a_kernel(qt_ref, k_ref, vt0_ref, vt1_ref, g_ref, o_ref, s0_ref, s1_ref, m1_ref, half_ref):
    group = A_Q_HEADS // A_KV_HEADS
    width = group * TQ_A
    S = k_ref.shape[0]
    top = lax.broadcasted_iota(jnp.int32, (LANES, TQ_A), 0) < HEAD_DIM
    ones = jnp.ones((16, KEY_CHUNK), BF16)

    @pl.when(pl.program_id(0) == 0)
    def _():
        s1_ref[...] = jnp.zeros_like(s1_ref)
        m1_ref[...] = jnp.zeros_like(m1_ref)
        half_ref[...] = jnp.zeros_like(half_ref)

    def half_step(kv, s_new_ref, s_old_ref, old_max, vt_old_ref, side_work=None):
        cols = []
        for j in range(group // 2):
            r0 = (kv * group + 2 * j) * HEAD_DIM
            qt2 = qt_ref[r0:r0 + LANES, :]
            zero = jnp.zeros_like(qt2)
            cols += [jnp.where(top, qt2, zero), jnp.where(top, zero, qt2)]
        w = jnp.concatenate(cols, axis=1)
        run_max = jnp.full((8, width), -jnp.inf, F32)
        acc = jnp.zeros((HEAD_DIM + 16, width), F32)
        for c in range(S // KEY_CHUNK):
            rows = slice(c * KEY_CHUNK, (c + 1) * KEY_CHUNK)
            st = jnp.dot(k_ref[rows, kv * LANES:(kv + 1) * LANES], w,
                         preferred_element_type=F32)
            p = jnp.exp2(s_old_ref[rows, :] - old_max).astype(BF16)
            vaug = jnp.concatenate([vt_old_ref[:, rows], ones], axis=0)
            acc = acc + jnp.dot(vaug, p, preferred_element_type=F32)
            s_new_ref[rows, :] = st
            run_max = jnp.maximum(run_max, st.reshape(KEY_CHUNK // 8, 8, width).max(axis=0))
            if side_work is not None and c == 0:
                side_work()
        ot = acc[0:HEAD_DIM] * (1.0 / acc[HEAD_DIM:HEAD_DIM + 1])
        return jnp.max(run_max, axis=0, keepdims=True), ot

    def head_blocks(ot):
        return [jnp.concatenate([ot[:, 2 * j * TQ_A:(2 * j + 1) * TQ_A],
                                 ot[:, (2 * j + 1) * TQ_A:(2 * j + 2) * TQ_A]], axis=0).T
                for j in range(group // 2)]

    m0, late = half_step(0, s0_ref, s1_ref, m1_ref[...], vt1_ref)

    def write_previous_block():
        y = jnp.concatenate(head_blocks(half_ref[...]) + head_blocks(late), axis=1)
        o_ref[...] = _rms(y, g_ref[...]).astype(BF16)

    m1, early = half_step(1, s1_ref, s0_ref, m0, vt0_ref, write_previous_block)
    m1_ref[...] = m1
    half_ref[...] = early


def _attention_a(qat, ka, vat, g):
    B, _, S = qat.shape
    assert A_KV_HEADS == 2, "the step skew pairs exactly two kv groups"
    width = (A_Q_HEADS // A_KV_HEADS) * TQ_A
    nq = S // TQ_A
    n_steps = B * nq
    cur = lambda t: jnp.minimum(t, n_steps - 1)
    prev = lambda t: jnp.maximum(t - 1, 0)
    return pl.pallas_call(
        _attn_a_kernel,
        grid=(n_steps + 1,),
        in_specs=[
            pl.BlockSpec((None, A_WIDTH, TQ_A), lambda t: (cur(t) // nq, 0, cur(t) % nq)),
            pl.BlockSpec((None, S, 2 * A_KV_WIDTH), lambda t: (cur(t) // nq, 0, 0)),
            pl.BlockSpec((None, HEAD_DIM, S), lambda t: (cur(t) // nq, 0, 0)),
            pl.BlockSpec((None, HEAD_DIM, S), lambda t: (prev(t) // nq, 1, 0)),
            pl.BlockSpec((1, A_WIDTH), lambda t: (0, 0)),
        ],
        out_specs=pl.BlockSpec((None, TQ_A, A_WIDTH), lambda t: (prev(t) // nq, prev(t) % nq, 0)),
        out_shape=jax.ShapeDtypeStruct((B, S, A_WIDTH), BF16),
        scratch_shapes=[pltpu.VMEM((S, width), F32), pltpu.VMEM((S, width), F32),
                        pltpu.VMEM((1, width), F32), pltpu.VMEM((HEAD_DIM, width), F32)],
        compiler_params=pltpu.CompilerParams(
            dimension_semantics=("arbitrary",), vmem_limit_bytes=VMEM_LIMIT),
        name="attention_a",
    )(qat, ka, vat, vat, g)


def _t5_bucket(rel):
    nb_half = N_BUCKETS // 2
    max_exact = nb_half // 2
    ret = jnp.where(rel > 0, nb_half, 0)
    n = jnp.abs(rel)
    large = max_exact + (jnp.log(jnp.maximum(n, 1).astype(F32) / max_exact)
                         / math.log(BUCKET_MAX_DIST / max_exact)
                         * (nb_half - max_exact)).astype(jnp.int32)
    large = jnp.minimum(large, nb_half - 1)
    return ret + jnp.where(n < max_exact, n, large)


def _variant_offset(v):
    return (0, -HALF_WINDOW, -2 * HALF_WINDOW)[v]


def _bias_kernel(rbt_ref, code_ref, valid_ref, o_ref):
    h = pl.program_id(0)
    rb = rbt_ref[pl.ds(h, 1), :]
    for n in range(len(DIL_BRANCHES)):
        code = code_ref[n:n + 1, :]
        acc = jnp.zeros(code.shape, F32)
        for b in range(N_BUCKETS):
            acc = jnp.where(code == b, rb[:, b:b + 1], acc)
        f = jnp.where(valid_ref[n:n + 1, :] > 0, acc * LOG2E, NEG)
        rows = jnp.broadcast_to(f, (TQ, BIAS_SPAN))
        for v in range(N_VARIANTS):
            shift = (KEY_WINDOW - _variant_offset(v)) % BIAS_SPAN
            o_ref[n * N_VARIANTS + v] = pltpu.roll(rows, shift, 1, stride=1, stride_axis=0)[:, 0:KEY_WINDOW]


def _bias_blocks(rel_bias):
    rel = jnp.arange(BIAS_SPAN, dtype=jnp.int32) - KEY_WINDOW
    code = jnp.stack([_t5_bucket(rel * r) for _, r in DIL_BRANCHES])
    valid = jnp.broadcast_to((jnp.abs(rel) <= HALF_WINDOW).astype(jnp.int32), code.shape)
    n_blocks = len(DIL_BRANCHES) * N_VARIANTS
    const = lambda h: (0, 0)
    return pl.pallas_call(
        _bias_kernel,
        grid=(B_HEADS,),
        in_specs=[pl.BlockSpec((B_HEADS, N_BUCKETS), const),
                  pl.BlockSpec(code.shape, const),
                  pl.BlockSpec(code.shape, const)],
        out_specs=pl.BlockSpec((n_blocks, None, TQ, KEY_WINDOW), lambda h: (0, h, 0, 0)),
        out_shape=jax.ShapeDtypeStruct((n_blocks, B_HEADS, TQ, KEY_WINDOW), F32),
        compiler_params=pltpu.CompilerParams(dimension_semantics=("arbitrary",)),
        name="bias_blocks",
    )(rel_bias.T, code, valid)


def _split_pair(q2):
    lane = lax.broadcasted_iota(jnp.int32, q2.shape, 1)
    low = lane < HEAD_DIM
    zero = jnp.zeros_like(q2)
    return jnp.concatenate([jnp.where(low, q2, zero), jnp.where(low, zero, q2)], axis=0)


def _attn_b_kernel(q1_ref, q4_ref, q16_ref, k1_ref, k4_ref, k16_ref, v1_ref, v4_ref, v16_ref,
                   bias_ref, g_ref, o_ref, y_ref, *acc_refs):
    pair = pl.program_id(1)
    S = q1_ref.shape[0]
    n_blocks = S // TQ
    low = lax.broadcasted_iota(jnp.int32, (TQ, LANES), 1) < HEAD_DIM
    q_refs = (q1_ref, q4_ref, q16_ref)
    k_refs = (k1_ref, k4_ref, k16_ref)
    v_refs = (v1_ref, v4_ref, v16_ref)

    def merge(x2):
        return jnp.where(low, x2[0:TQ], x2[TQ:2 * TQ])

    def banded_block(q2, k2, v2, bias):
        s = lax.dot_general(_split_pair(q2), k2, (((1,), (1,)), ((), ())),
                            preferred_element_type=F32) + bias
        m = jnp.max(s, axis=1, keepdims=True)
        p = jnp.exp2(s - m).astype(BF16)
        vaug = jnp.concatenate([v2, jnp.ones_like(v2)], axis=1)
        oa = jnp.dot(p, vaug, preferred_element_type=F32)
        den = merge(oa[:, LANES:2 * LANES])
        out = merge(oa[:, 0:LANES]) * (1.0 / den)
        lse = merge(jnp.broadcast_to(m, (2 * TQ, LANES))) + jnp.log2(den)
        return out, lse

    def step(i, carry):
        for n, (_, r) in enumerate(DIL_BRANCHES):
            length = S // r
            per_phase = length // TQ
            if per_phase > 1:
                ph = i // per_phase
                blk = i - ph * per_phase
                variant = jnp.where(blk == 0, 0, jnp.where(blk == per_phase - 1, 2, 1))
                r0 = pl.multiple_of(blk * TQ, TQ)
                w0 = pl.multiple_of(jnp.clip(blk * TQ - HALF_WINDOW, 0, length - KEY_WINDOW), HALF_WINDOW)
                width = KEY_WINDOW
            else:
                ph, blk, variant, r0, w0, width = i, 0, 0, 0, 0, length
            tab = n * N_VARIANTS + variant
            bias = jnp.concatenate([bias_ref[tab, 2 * pair, :, 0:width],
                                    bias_ref[tab, 2 * pair + 1, :, 0:width]], axis=0)
            if r == 1:
                q2 = q_refs[n][pl.ds(r0, TQ), :]
                k2 = k_refs[n][pl.ds(w0, width), :]
                v2 = v_refs[n][pl.ds(w0, width), :]
            else:
                q2 = q_refs[n][ph, pl.ds(r0, TQ), :]
                k2 = k_refs[n][ph, pl.ds(w0, width), :]
                v2 = v_refs[n][ph, pl.ds(w0, width), :]
            out, lse = banded_block(q2, k2, v2, bias)
            rows = pl.ds(r0 * r + ph, TQ, stride=r) if r > 1 else pl.ds(r0, TQ)
            acc_refs[2 * n][rows, :] = out
            acc_refs[2 * n + 1][rows, :] = lse
        return carry

    lax.fori_loop(0, n_blocks, step, 0, unroll=B_UNROLL)

    lses = [acc_refs[2 * n + 1][...] for n in range(len(DIL_BRANCHES))]
    top = functools.reduce(jnp.maximum, lses)
    weights = [jnp.exp2(l - top) for l in lses]
    num = sum(w * acc_refs[2 * n][...] for n, w in enumerate(weights))
    y = num * (1.0 / sum(weights))
    for c in range(B_HEADS // 2):
        @pl.when(pair == c)
        def _(c=c):
            y_ref[:, c * LANES:(c + 1) * LANES] = y

    @pl.when(pair == B_HEADS // 2 - 1)
    def _():
        o_ref[...] = _rms(y_ref[...], g_ref[...]).astype(BF16)


def _attention_b(orders, bias, g):
    B, S, _ = orders[0].shape
    specs = []
    for a in orders:
        if a.ndim == 3:
            specs.append(pl.BlockSpec((None, S, LANES), lambda b, j: (b, 0, j)))
        else:
            specs.append(pl.BlockSpec((None, a.shape[1], a.shape[2], LANES), lambda b, j: (b, 0, 0, j)))
    return pl.pallas_call(
        _attn_b_kernel,
        grid=(B, B_HEADS // 2),
        in_specs=specs + [
            pl.BlockSpec(bias.shape, lambda b, j: (0, 0, 0, 0), pipeline_mode=pl.Buffered(1)),
            pl.BlockSpec((1, B_WIDTH), lambda b, j: (0, 0)),
        ],
        out_specs=pl.BlockSpec((None, S, B_WIDTH), lambda b, j: (b, 0, 0)),
        out_shape=jax.ShapeDtypeStruct((B, S, B_WIDTH), BF16),
        scratch_shapes=[pltpu.VMEM((S, B_WIDTH), F32)] + [pltpu.VMEM((S, LANES), F32)] * 6,
        compiler_params=pltpu.CompilerParams(
            dimension_semantics=("arbitrary", "arbitrary"), vmem_limit_bytes=VMEM_LIMIT),
        name="attention_b",
    )(*orders, bias, g)


def _out_mlp_kernel(x_ref, ma_ref, mb_ref, wo_ref, gm_ref, wup_ref, wdn_ref, gf_ref, o_ref):
    mix = jnp.concatenate([ma_ref[...], mb_ref[...]], axis=1)
    h = x_ref[...] + jnp.dot(mix, wo_ref[...], preferred_element_type=F32)
    hn = _rms(h, gm_ref[...]).astype(BF16)
    acc = h
    d_ff = wup_ref.shape[1]
    for c in range(d_ff // FF_CHUNK):
        sl = slice(c * FF_CHUNK, (c + 1) * FF_CHUNK)
        u = jnp.maximum(jnp.dot(hn, wup_ref[:, sl], preferred_element_type=F32), 0.0)
        acc = acc + jnp.dot((u * u).astype(BF16), wdn_ref[sl, :], preferred_element_type=F32)
    o_ref[...] = _rms(acc, gf_ref[...])


def _out_mlp(x2, ma, mb, wo, gm, wup, wdn, gf):
    N, D = x2.shape
    d_ff = wup.shape[1]
    tok = lambda t: (t, 0)
    const = lambda t: (0, 0)
    resident = functools.partial(pl.BlockSpec, index_map=const, pipeline_mode=pl.Buffered(1))
    return pl.pallas_call(
        _out_mlp_kernel,
        grid=(N // TM_OUT,),
        in_specs=[
            pl.BlockSpec((TM_OUT, D), tok),
            pl.BlockSpec((TM_OUT, A_WIDTH), tok),
            pl.BlockSpec((TM_OUT, B_WIDTH), tok),
            resident((A_WIDTH + B_WIDTH, D)),
            pl.BlockSpec((1, D), const),
            resident((D, d_ff)),
            resident((d_ff, D)),
            pl.BlockSpec((1, D), const),
        ],
        out_specs=pl.BlockSpec((TM_OUT, D), tok),
        out_shape=jax.ShapeDtypeStruct((N, D), F32),
        compiler_params=pltpu.CompilerParams(
            dimension_semantics=("arbitrary",), vmem_limit_bytes=VMEM_LIMIT),
        name="out_mlp",
    )(x2, ma, mb, wo, gm, wup, wdn, gf)


def kernel(x, attn_norm_g, w_in, q_norm_g, k_norm_g, rel_bias, out_norm_a_g, out_norm_b_g,
           w_out, mlp_norm_g, w_up, w_down, final_norm_g):
    B, S, D = x.shape
    assert w_in.shape[0] == 1, "the output kernel fuses the final norm: single layer only"
    bias = _bias_blocks(rel_bias)
    gq = jnp.tile(q_norm_g[0], LANES // HEAD_DIM).reshape(1, LANES)
    gk = jnp.tile(k_norm_g[0], LANES // HEAD_DIM).reshape(1, LANES)
    qat, ka, vat, *orders = _in_projection(
        x, attn_norm_g[0].reshape(1, D), w_in[0].astype(BF16), gq, gk)
    ma = _attention_a(qat, ka, vat, out_norm_a_g[0].reshape(1, A_WIDTH))
    mb = _attention_b(orders, bias, out_norm_b_g[0].reshape(1, B_WIDTH))
    out = _out_mlp(x.reshape(B * S, D), ma.reshape(B * S, A_WIDTH), mb.reshape(B * S, B_WIDTH),
                   w_out[0].astype(BF16), mlp_norm_g[0].reshape(1, D),
                   w_up[0].astype(BF16), w_down[0].astype(BF16), final_norm_g.reshape(1, D))
    return out.reshape(B, S, D)
```

```python
import functools
import math

import jax
import jax.numpy as jnp
import numpy as np
from jax import lax
from jax.experimental import pallas as pl
from jax.experimental.pallas import tpu as pltpu

HEAD_DIM = 64
A_Q_HEADS = 8
A_KV_HEADS = 2
B_HEADS = 8
GRID_W = 64
ROPE_THETA = 10000.0
DIL_BRANCHES = ((128, 1), (512, 4), (2048, 16))
N_BUCKETS = 32
BUCKET_MAX_DIST = 1024
EPS = 1e-6
NEG = -1e30

A_WIDTH = A_Q_HEADS * HEAD_DIM
A_KV_WIDTH = A_KV_HEADS * HEAD_DIM
B_WIDTH = B_HEADS * HEAD_DIM
LOG2E = math.log2(math.e)
Q_SCALE = HEAD_DIM ** -0.5 * LOG2E

LANES = 128
TM_IN = 512
TQ = 128
TQ_A = 256
KEY_CHUNK = 256
TM_OUT = 1024
FF_CHUNK = 1024
VMEM_LIMIT = 56 * 1024 * 1024

HALF_WINDOW = DIL_BRANCHES[0][0] // (2 * DIL_BRANCHES[0][1])
assert all(w // (2 * r) == HALF_WINDOW for w, r in DIL_BRANCHES)
KEY_WINDOW = TQ + 2 * HALF_WINDOW
BIAS_SPAN = 2 * KEY_WINDOW
N_VARIANTS = 3
B_UNROLL = 16

F32 = jnp.float32
BF16 = jnp.bfloat16


def _rms(x, g):
    return x * lax.rsqrt(jnp.mean(x * x, axis=-1, keepdims=True) + EPS) * g


def _inproj_kernel(x_ref, g_ref, w_ref, gq_ref, gk_ref, cos_ref, sin_ref, ones_ref,
                   qat_ref, ka_ref, vat_ref,
                   q1_ref, q4_ref, q16_ref, k1_ref, k4_ref, k16_ref, v1_ref, v4_ref, v16_ref,
                   sq_ref, sk_ref, sv_ref):
    n = _rms(x_ref[...], g_ref[...]).astype(BF16)
    cos = cos_ref[...]
    sin = sin_ref[...]
    ones2 = ones_ref[...]
    lane = lax.broadcasted_iota(jnp.int32, (TM_IN, LANES), 1)
    first_half = (lane % 32) < 16

    def head_norm_rope(blk, gain):
        sq = blk * blk
        hi = sq.astype(BF16)
        lo = (sq - hi.astype(F32)).astype(BF16)
        ss = jnp.dot(jnp.concatenate([hi, lo], axis=1), ones2, preferred_element_type=F32)
        xn = blk * lax.rsqrt(ss * (1.0 / HEAD_DIM) + EPS) * gain
        partner = jnp.where(first_half, pltpu.roll(xn, LANES - 16, 1), pltpu.roll(xn, 16, 1))
        return xn * cos + partner * sin

    o_ka = A_WIDTH
    o_va = o_ka + A_KV_WIDTH
    o_qb = o_va + A_KV_WIDTH
    o_kb = o_qb + B_WIDTH
    o_vb = o_kb + B_WIDTH

    def stash(val, scr_ref, nat_ref):
        nat_ref[...] = val.astype(BF16)
        for c in range(B_WIDTH // LANES):
            scr_ref[c] = val[:, c * LANES:(c + 1) * LANES]

    def write_phases(scr_ref, ph_refs):
        for c in range(B_WIDTH // LANES):
            sl = slice(c * LANES, (c + 1) * LANES)
            for (_, r), ph_ref in zip(DIL_BRANCHES[1:], ph_refs):
                for p in range(r):
                    ph_ref[p, :, sl] = scr_ref[c, pl.ds(p, TM_IN // r, stride=r), :].astype(BF16)

    pq = jnp.dot(n, w_ref[:, 0:o_ka], preferred_element_type=F32)
    stash(jnp.dot(n, w_ref[:, o_qb:o_kb], preferred_element_type=F32) * Q_SCALE, sq_ref, q1_ref)
    gq = gq_ref[...]
    for j in range(A_WIDTH // LANES):
        sl = slice(j * LANES, (j + 1) * LANES)
        qat_ref[sl, :] = (head_norm_rope(pq[:, sl], gq) * Q_SCALE).T.astype(BF16)

    pk = jnp.dot(n, w_ref[:, o_ka:o_va], preferred_element_type=F32)
    pv = jnp.dot(n, w_ref[:, o_va:o_qb], preferred_element_type=F32)
    stash(jnp.dot(n, w_ref[:, o_kb:o_vb], preferred_element_type=F32), sk_ref, k1_ref)
    kr = head_norm_rope(pk, gk_ref[...])
    swapped = pltpu.roll(kr, HEAD_DIM, 1)
    low = lane < HEAD_DIM
    ka_ref[:, 0:LANES] = jnp.where(low, kr, swapped).astype(BF16)
    ka_ref[:, LANES:2 * LANES] = jnp.where(low, swapped, kr).astype(BF16)
    vat_ref[...] = pv.T.astype(BF16)
    write_phases(sq_ref, (q4_ref, q16_ref))

    stash(jnp.dot(n, w_ref[:, o_vb:], preferred_element_type=F32), sv_ref, v1_ref)
    write_phases(sk_ref, (k4_ref, k16_ref))
    write_phases(sv_ref, (v4_ref, v16_ref))


def _rope_tables(S):
    rows = S // GRID_W
    row = jnp.repeat(jnp.arange(rows, dtype=F32), GRID_W)
    col = jnp.tile(jnp.arange(GRID_W, dtype=F32), rows)
    n_freq = HEAD_DIM // 4
    inv = ROPE_THETA ** (-jnp.arange(n_freq, dtype=F32) / n_freq)
    ang_r = row[:, None] * inv[None, :]
    ang_c = col[:, None] * inv[None, :]
    cos_h = jnp.concatenate([jnp.cos(ang_r)] * 2 + [jnp.cos(ang_c)] * 2, axis=1)
    sin_h = jnp.concatenate([-jnp.sin(ang_r), jnp.sin(ang_r), -jnp.sin(ang_c), jnp.sin(ang_c)], axis=1)
    return jnp.tile(cos_h, (1, 2)), jnp.tile(sin_h, (1, 2))


def _in_projection(x, g, w_bf16, gq, gk):
    B, S, D = x.shape
    in_width = w_bf16.shape[1]
    cos, sin = _rope_tables(S)
    r = np.arange(2 * LANES)[:, None]
    c = np.arange(LANES)[None, :]
    ones2 = jnp.asarray(((r % LANES) // HEAD_DIM) == (c // HEAD_DIM), dtype=BF16)
    nt = S // TM_IN
    tok = lambda t, b: (b, t, 0)
    phase = lambda t, b: (b, 0, t, 0)
    const = lambda t, b: (0, 0)
    orders_shape, orders_spec = [], []
    for _ in range(3):
        orders_shape.append(jax.ShapeDtypeStruct((B, S, B_WIDTH), BF16))
        orders_spec.append(pl.BlockSpec((None, TM_IN, B_WIDTH), tok))
        for _, rr in DIL_BRANCHES[1:]:
            orders_shape.append(jax.ShapeDtypeStruct((B, rr, S // rr, B_WIDTH), BF16))
            orders_spec.append(pl.BlockSpec((None, rr, TM_IN // rr, B_WIDTH), phase))
    out_shape = (
        jax.ShapeDtypeStruct((B, A_WIDTH, S), BF16),
        jax.ShapeDtypeStruct((B, S, 2 * A_KV_WIDTH), BF16),
        jax.ShapeDtypeStruct((B, A_KV_WIDTH, S), BF16),
        *orders_shape,
    )
    return pl.pallas_call(
        _inproj_kernel,
        grid=(nt, B),
        in_specs=[
            pl.BlockSpec((None, TM_IN, D), tok),
            pl.BlockSpec((1, D), const),
            pl.BlockSpec((D, in_width), const),
            pl.BlockSpec((1, LANES), const),
            pl.BlockSpec((1, LANES), const),
            pl.BlockSpec((TM_IN, LANES), lambda t, b: (t, 0)),
            pl.BlockSpec((TM_IN, LANES), lambda t, b: (t, 0)),
            pl.BlockSpec((2 * LANES, LANES), const),
        ],
        out_specs=(
            pl.BlockSpec((None, A_WIDTH, TM_IN), lambda t, b: (b, 0, t)),
            pl.BlockSpec((None, TM_IN, 2 * A_KV_WIDTH), tok),
            pl.BlockSpec((None, A_KV_WIDTH, TM_IN), lambda t, b: (b, 0, t)),
            *orders_spec,
        ),
        out_shape=out_shape,
        scratch_shapes=[pltpu.VMEM((B_WIDTH // LANES, TM_IN, LANES), F32)] * 3,
        compiler_params=pltpu.CompilerParams(
            dimension_semantics=("arbitrary", "arbitrary"), vmem_limit_bytes=VMEM_LIMIT),
        name="in_projection",
    )(x, g, w_bf16, gq, gk, cos, sin, ones2)


def _attn_a_kernel(qt_ref, k_ref, vt0_ref, vt1_ref, g_ref, o_ref, s0_ref, s1_ref, m1_ref, half_ref):
    group = A_Q_HEADS // A_KV_HEADS
    width = group * TQ_A
    S = k_ref.shape[0]
    top = lax.broadcasted_iota(jnp.int32, (LANES, TQ_A), 0) < HEAD_DIM
    ones = jnp.ones((16, KEY_CHUNK), BF16)

    @pl.when(pl.program_id(0) == 0)
    def _():
        s1_ref[...] = jnp.zeros_like(s1_ref)
        m1_ref[...] = jnp.zeros_like(m1_ref)
        half_ref[...] = jnp.zeros_like(half_ref)

    def half_step(kv, s_new_ref, s_old_ref, old_max, vt_old_ref, side_work=None):
        cols = []
        for j in range(group // 2):
            r0 = (kv * group + 2 * j) * HEAD_DIM
            qt2 = qt_ref[r0:r0 + LANES, :]
            zero = jnp.zeros_like(qt2)
            cols += [jnp.where(top, qt2, zero), jnp.where(top, zero, qt2)]
        w = jnp.concatenate(cols, axis=1)
        run_max = jnp.full((8, width), -jnp.inf, F32)
        acc = jnp.zeros((HEAD_DIM + 16, width), F32)
        for c in range(S // KEY_CHUNK):
            rows = slice(c * KEY_CHUNK, (c + 1) * KEY_CHUNK)
            st = jnp.dot(k_ref[rows, kv * LANES:(kv + 1) * LANES], w,
                         preferred_element_type=F32)
            p = jnp.exp2(s_old_ref[rows, :] - old_max).astype(BF16)
            vaug = jnp.concatenate([vt_old_ref[:, rows], ones], axis=0)
            acc = acc + jnp.dot(vaug, p, preferred_element_type=F32)
            s_new_ref[rows, :] = st
            run_max = jnp.maximum(run_max, st.reshape(KEY_CHUNK // 8, 8, width).max(axis=0))
            if side_work is not None and c == 0:
                side_work()
        ot = acc[0:HEAD_DIM] * (1.0 / acc[HEAD_DIM:HEAD_DIM + 1])
        return jnp.max(run_max, axis=0, keepdims=True), ot

    def head_blocks(ot):
        return [jnp.concatenate([ot[:, 2 * j * TQ_A:(2 * j + 1) * TQ_A],
                                 ot[:, (2 * j + 1) * TQ_A:(2 * j + 2) * TQ_A]], axis=0).T
                for j in range(group // 2)]

    m0, late = half_step(0, s0_ref, s1_ref, m1_ref[...], vt1_ref)

    def write_previous_block():
        y = jnp.concatenate(head_blocks(half_ref[...]) + head_blocks(late), axis=1)
        o_ref[...] = _rms(y, g_ref[...]).astype(BF16)

    m1, early = half_step(1, s1_ref, s0_ref, m0, vt0_ref, write_previous_block)
    m1_ref[...] = m1
    half_ref[...] = early


def _attention_a(qat, ka, vat, g):
    B, _, S = qat.shape
    assert A_KV_HEADS == 2, "the step skew pairs exactly two kv groups"
    width = (A_Q_HEADS // A_KV_HEADS) * TQ_A
    nq = S // TQ_A
    n_steps = B * nq
    cur = lambda t: jnp.minimum(t, n_steps - 1)
    prev = lambda t: jnp.maximum(t - 1, 0)
    return pl.pallas_call(
        _attn_a_kernel,
        grid=(n_steps + 1,),
        in_specs=[
            pl.BlockSpec((None, A_WIDTH, TQ_A), lambda t: (cur(t) // nq, 0, cur(t) % nq)),
            pl.BlockSpec((None, S, 2 * A_KV_WIDTH), lambda t: (cur(t) // nq, 0, 0)),
            pl.BlockSpec((None, HEAD_DIM, S), lambda t: (cur(t) // nq, 0, 0)),
            pl.BlockSpec((None, HEAD_DIM, S), lambda t: (prev(t) // nq, 1, 0)),
            pl.BlockSpec((1, A_WIDTH), lambda t: (0, 0)),
        ],
        out_specs=pl.BlockSpec((None, TQ_A, A_WIDTH), lambda t: (prev(t) // nq, prev(t) % nq, 0)),
        out_shape=jax.ShapeDtypeStruct((B, S, A_WIDTH), BF16),
        scratch_shapes=[pltpu.VMEM((S, width), F32), pltpu.VMEM((S, width), F32),
                        pltpu.VMEM((1, width), F32), pltpu.VMEM((HEAD_DIM, width), F32)],
        compiler_params=pltpu.CompilerParams(
            dimension_semantics=("arbitrary",), vmem_limit_bytes=VMEM_LIMIT),
        name="attention_a",
    )(qat, ka, vat, vat, g)


def _t5_bucket(rel):
    rel = np.asarray(rel, np.int32)
    nb_half = N_BUCKETS // 2
    max_exact = nb_half // 2
    ret = np.where(rel > 0, nb_half, 0)
    n = np.abs(rel)
    large = max_exact + (np.log(np.maximum(n, 1).astype(np.float32) / max_exact)
                         / math.log(BUCKET_MAX_DIST / max_exact)
                         * (nb_half - max_exact)).astype(np.int32)
    large = np.minimum(large, nb_half - 1)
    return (ret + np.where(n < max_exact, n, large)).astype(np.int32)


def _variant_offset(v):
    return (0, -HALF_WINDOW, -2 * HALF_WINDOW)[v]


def _bias_kernel(rbt_ref, code_ref, valid_ref, o_ref):
    h = pl.program_id(0)
    rb = rbt_ref[pl.ds(h, 1), :]
    for n in range(len(DIL_BRANCHES)):
        code = code_ref[n:n + 1, :]
        acc = jnp.zeros(code.shape, F32)
        for b in range(N_BUCKETS):
            acc = jnp.where(code == b, rb[:, b:b + 1], acc)
        f = jnp.where(valid_ref[n:n + 1, :] > 0, acc * LOG2E, NEG)
        rows = jnp.broadcast_to(f, (TQ, BIAS_SPAN))
        for v in range(N_VARIANTS):
            shift = (KEY_WINDOW - _variant_offset(v)) % BIAS_SPAN
            o_ref[n * N_VARIANTS + v] = pltpu.roll(rows, shift, 1, stride=1, stride_axis=0)[:, 0:KEY_WINDOW]


def _bias_blocks(rel_bias):
    rel = np.arange(BIAS_SPAN, dtype=np.int32) - KEY_WINDOW
    code = np.stack([_t5_bucket(rel * r) for _, r in DIL_BRANCHES])
    valid = np.broadcast_to((np.abs(rel) <= HALF_WINDOW).astype(np.int32), code.shape)
    n_blocks = len(DIL_BRANCHES) * N_VARIANTS
    const = lambda h: (0, 0)
    return pl.pallas_call(
        _bias_kernel,
        grid=(B_HEADS,),
        in_specs=[pl.BlockSpec((B_HEADS, N_BUCKETS), const),
                  pl.BlockSpec(code.shape, const),
                  pl.BlockSpec(code.shape, const)],
        out_specs=pl.BlockSpec((n_blocks, None, TQ, KEY_WINDOW), lambda h: (0, h, 0, 0)),
        out_shape=jax.ShapeDtypeStruct((n_blocks, B_HEADS, TQ, KEY_WINDOW), F32),
        compiler_params=pltpu.CompilerParams(dimension_semantics=("arbitrary",)),
        name="bias_blocks",
    )(rel_bias.T, jnp.asarray(code), jnp.asarray(valid))


def _split_pair(q2):
    lane = lax.broadcasted_iota(jnp.int32, q2.shape, 1)
    low = lane < HEAD_DIM
    zero = jnp.zeros_like(q2)
    return jnp.concatenate([jnp.where(low, q2, zero), jnp.where(low, zero, q2)], axis=0)


def _attn_b_kernel(q1_ref, q4_ref, q16_ref, k1_ref, k4_ref, k16_ref, v1_ref, v4_ref, v16_ref,
                   bias_ref, g_ref, o_ref, y_ref, *acc_refs):
    pair = pl.program_id(1)
    S = q1_ref.shape[0]
    n_blocks = S // TQ
    low = lax.broadcasted_iota(jnp.int32, (TQ, LANES), 1) < HEAD_DIM
    q_refs = (q1_ref, q4_ref, q16_ref)
    k_refs = (k1_ref, k4_ref, k16_ref)
    v_refs = (v1_ref, v4_ref, v16_ref)

    def merge(x2):
        return jnp.where(low, x2[0:TQ], x2[TQ:2 * TQ])

    def banded_block(q2, k2, v2, bias):
        s = lax.dot_general(_split_pair(q2), k2, (((1,), (1,)), ((), ())),
                            preferred_element_type=F32) + bias
        m = jnp.max(s, axis=1, keepdims=True)
        p = jnp.exp2(s - m).astype(BF16)
        vaug = jnp.concatenate([v2, jnp.ones_like(v2)], axis=1)
        oa = jnp.dot(p, vaug, preferred_element_type=F32)
        den = merge(oa[:, LANES:2 * LANES])
        out = merge(oa[:, 0:LANES]) * (1.0 / den)
        lse = merge(jnp.broadcast_to(m, (2 * TQ, LANES))) + jnp.log2(den)
        return out, lse

    def step(i, carry):
        for n, (_, r) in enumerate(DIL_BRANCHES):
            length = S // r
            per_phase = length // TQ
            if per_phase > 1:
                ph = i // per_phase
                blk = i - ph * per_phase
                variant = jnp.where(blk == 0, 0, jnp.where(blk == per_phase - 1, 2, 1))
                r0 = pl.multiple_of(blk * TQ, TQ)
                w0 = pl.multiple_of(jnp.clip(blk * TQ - HALF_WINDOW, 0, length - KEY_WINDOW), HALF_WINDOW)
                width = KEY_WINDOW
            else:
                ph, blk, variant, r0, w0, width = i, 0, 0, 0, 0, length
            tab = n * N_VARIANTS + variant
            bias = jnp.concatenate([bias_ref[tab, 2 * pair, :, 0:width],
                                    bias_ref[tab, 2 * pair + 1, :, 0:width]], axis=0)
            if r == 1:
                q2 = q_refs[n][pl.ds(r0, TQ), :]
                k2 = k_refs[n][pl.ds(w0, width), :]
                v2 = v_refs[n][pl.ds(w0, width), :]
            else:
                q2 = q_refs[n][ph, pl.ds(r0, TQ), :]
                k2 = k_refs[n][ph, pl.ds(w0, width), :]
                v2 = v_refs[n][ph, pl.ds(w0, width), :]
            out, lse = banded_block(q2, k2, v2, bias)
            rows = pl.ds(r0 * r + ph, TQ, stride=r) if r > 1 else pl.ds(r0, TQ)
            acc_refs[2 * n][rows, :] = out
            acc_refs[2 * n + 1][rows, :] = lse
        return carry

    lax.fori_loop(0, n_blocks, step, 0, unroll=B_UNROLL)

    lses = [acc_refs[2 * n + 1][...] for n in range(len(DIL_BRANCHES))]
    top = functools.reduce(jnp.maximum, lses)
    weights = [jnp.exp2(l - top) for l in lses]
    num = sum(w * acc_refs[2 * n][...] for n, w in enumerate(weights))
    y = num * (1.0 / sum(weights))
    for c in range(B_HEADS // 2):
        @pl.when(pair == c)
        def _(c=c):
            y_ref[:, c * LANES:(c + 1) * LANES] = y

    @pl.when(pair == B_HEADS // 2 - 1)
    def _():
        o_ref[...] = _rms(y_ref[...], g_ref[...]).astype(BF16)


def _attention_b(orders, bias, g):
    B, S, _ = orders[0].shape
    specs = []
    for a in orders:
        if a.ndim == 3:
            specs.append(pl.BlockSpec((None, S, LANES), lambda b, j: (b, 0, j)))
        else:
            specs.append(pl.BlockSpec((None, a.shape[1], a.shape[2], LANES), lambda b, j: (b, 0, 0, j)))
    return pl.pallas_call(
        _attn_b_kernel,
        grid=(B, B_HEADS // 2),
        in_specs=specs + [
            pl.BlockSpec(bias.shape, lambda b, j: (0, 0, 0, 0), pipeline_mode=pl.Buffered(1)),
            pl.BlockSpec((1, B_WIDTH), lambda b, j: (0, 0)),
        ],
        out_specs=pl.BlockSpec((None, S, B_WIDTH), lambda b, j: (b, 0, 0)),
        out_shape=jax.ShapeDtypeStruct((B, S, B_WIDTH), BF16),
        scratch_shapes=[pltpu.VMEM((S, B_WIDTH), F32)] + [pltpu.VMEM((S, LANES), F32)] * 6,
        compiler_params=pltpu.CompilerParams(
            dimension_semantics=("arbitrary", "arbitrary"), vmem_limit_bytes=VMEM_LIMIT),
        name="attention_b",
    )(*orders, bias, g)


def _out_mlp_kernel(x_ref, ma_ref, mb_ref, wo_ref, gm_ref, wup_ref, wdn_ref, gf_ref, o_ref):
    mix = jnp.concatenate([ma_ref[...], mb_ref[...]], axis=1)
    h = x_ref[...] + jnp.dot(mix, wo_ref[...], preferred_element_type=F32)
    hn = _rms(h, gm_ref[...]).astype(BF16)
    acc = h
    d_ff = wup_ref.shape[1]
    for c in range(d_ff // FF_CHUNK):
        sl = slice(c * FF_CHUNK, (c + 1) * FF_CHUNK)
        u = jnp.maximum(jnp.dot(hn, wup_ref[:, sl], preferred_element_type=F32), 0.0)
        acc = acc + jnp.dot((u * u).astype(BF16), wdn_ref[sl, :], preferred_element_type=F32)
    o_ref[...] = _rms(acc, gf_ref[...])


def _out_mlp(x2, ma, mb, wo, gm, wup, wdn, gf):
    N, D = x2.shape
    d_ff = wup.shape[1]
    tok = lambda t: (t, 0)
    const = lambda t: (0, 0)
    resident = functools.partial(pl.BlockSpec, index_map=const, pipeline_mode=pl.Buffered(1))
    return pl.pallas_call(
        _out_mlp_kernel,
        grid=(N // TM_OUT,),
        in_specs=[
            pl.BlockSpec((TM_OUT, D), tok),
            pl.BlockSpec((TM_OUT, A_WIDTH), tok),
            pl.BlockSpec((TM_OUT, B_WIDTH), tok),
            resident((A_WIDTH + B_WIDTH, D)),
            pl.BlockSpec((1, D), const),
            resident((D, d_ff)),
            resident((d_ff, D)),
            pl.BlockSpec((1, D), const),
        ],
        out_specs=pl.BlockSpec((TM_OUT, D), tok),
        out_shape=jax.ShapeDtypeStruct((N, D), F32),
        compiler_params=pltpu.CompilerParams(
            dimension_semantics=("arbitrary",), vmem_limit_bytes=VMEM_LIMIT),
        name="out_mlp",
    )(x2, ma, mb, wo, gm, wup, wdn, gf)


def kernel(x, attn_norm_g, w_in, q_norm_g, k_norm_g, rel_bias, out_norm_a_g, out_norm_b_g,
           w_out, mlp_norm_g, w_up, w_down, final_norm_g):
    B, S, D = x.shape
    assert w_in.shape[0] == 1, "the output kernel fuses the final norm: single layer only"
    bias = _bias_blocks(rel_bias)
    gq = jnp.tile(q_norm_g[0], LANES // HEAD_DIM).reshape(1, LANES)
    gk = jnp.tile(k_norm_g[0], LANES // HEAD_DIM).reshape(1, LANES)
    qat, ka, vat, *orders = _in_projection(
        x, attn_norm_g[0].reshape(1, D), w_in[0].astype(BF16), gq, gk)
    ma = _attention_a(qat, ka, vat, out_norm_a_g[0].reshape(1, A_WIDTH))
    mb = _attention_b(orders, bias, out_norm_b_g[0].reshape(1, B_WIDTH))
    out = _out_mlp(x.reshape(B * S, D), ma.reshape(B * S, A_WIDTH), mb.reshape(B * S, B_WIDTH),
                   w_out[0].astype(BF16), mlp_norm_g[0].reshape(1, D),
                   w_up[0].astype(BF16), w_down[0].astype(BF16), final_norm_g.reshape(1, D))
    return out.reshape(B, S, D)
```

```python
import functools
import math

import jax
import jax.numpy as jnp
import numpy as np
from jax import lax
from jax.experimental import pallas as pl
from jax.experimental.pallas import tpu as pltpu

HEAD_DIM = 64
A_Q_HEADS = 8
A_KV_HEADS = 2
B_HEADS = 8
GRID_W = 64
ROPE_THETA = 10000.0
DIL_BRANCHES = ((128, 1), (512, 4), (2048, 16))
N_BUCKETS = 32
BUCKET_MAX_DIST = 1024
EPS = 1e-6
NEG = -1e30

A_WIDTH = A_Q_HEADS * HEAD_DIM
A_KV_WIDTH = A_KV_HEADS * HEAD_DIM
B_WIDTH = B_HEADS * HEAD_DIM
LOG2E = math.log2(math.e)
Q_SCALE = HEAD_DIM ** -0.5 * LOG2E

LANES = 128
SUBLANES = 8
BF16_ROWS = 16
ROT = HEAD_DIM // 4
TM_IN = 512
TQ = 128
TQ_A = 256
KEY_CHUNK = 256
TM_OUT = 1024
FF_CHUNK = 1024
VMEM_LIMIT = 56 * 1024 * 1024

HALF_WINDOW = DIL_BRANCHES[0][0] // (2 * DIL_BRANCHES[0][1])
assert all(w // (2 * r) == HALF_WINDOW for w, r in DIL_BRANCHES)
KEY_WINDOW = TQ + 2 * HALF_WINDOW
BIAS_SPAN = 2 * KEY_WINDOW
N_VARIANTS = 3
B_UNROLL = 16

F32 = jnp.float32
BF16 = jnp.bfloat16


def _rms(x, g):
    return x * lax.rsqrt(jnp.mean(x * x, axis=-1, keepdims=True) + EPS) * g


def _inproj_kernel(x_ref, g_ref, w_ref, gq_ref, gk_ref, cos_ref, sin_ref, ones_ref,
                   qat_ref, ka_ref, vat_ref,
                   q1_ref, q4_ref, q16_ref, k1_ref, k4_ref, k16_ref, v1_ref, v4_ref, v16_ref,
                   sq_ref, sk_ref, sv_ref):
    n = _rms(x_ref[...], g_ref[...]).astype(BF16)
    cos = cos_ref[...]
    sin = sin_ref[...]
    ones2 = ones_ref[...]
    lane = lax.broadcasted_iota(jnp.int32, (TM_IN, LANES), 1)
    first_half = (lane % (2 * ROT)) < ROT

    def head_norm_rope(blk, gain):
        sq = blk * blk
        hi = sq.astype(BF16)
        lo = (sq - hi.astype(F32)).astype(BF16)
        ss = jnp.dot(jnp.concatenate([hi, lo], axis=1), ones2, preferred_element_type=F32)
        xn = blk * lax.rsqrt(ss * (1.0 / HEAD_DIM) + EPS) * gain
        partner = jnp.where(first_half, pltpu.roll(xn, LANES - ROT, 1), pltpu.roll(xn, ROT, 1))
        return xn * cos + partner * sin

    o_ka = A_WIDTH
    o_va = o_ka + A_KV_WIDTH
    o_qb = o_va + A_KV_WIDTH
    o_kb = o_qb + B_WIDTH
    o_vb = o_kb + B_WIDTH

    def stash(val, scr_ref, nat_ref):
        nat_ref[...] = val.astype(BF16)
        for c in range(B_WIDTH // LANES):
            scr_ref[c] = val[:, c * LANES:(c + 1) * LANES]

    def write_phases(scr_ref, ph_refs):
        for c in range(B_WIDTH // LANES):
            sl = slice(c * LANES, (c + 1) * LANES)
            for (_, r), ph_ref in zip(DIL_BRANCHES[1:], ph_refs):
                for p in range(r):
                    ph_ref[p, :, sl] = scr_ref[c, pl.ds(p, TM_IN // r, stride=r), :].astype(BF16)

    pq = jnp.dot(n, w_ref[:, 0:o_ka], preferred_element_type=F32)
    stash(jnp.dot(n, w_ref[:, o_qb:o_kb], preferred_element_type=F32) * Q_SCALE, sq_ref, q1_ref)
    gq = gq_ref[...]
    for j in range(A_WIDTH // LANES):
        sl = slice(j * LANES, (j + 1) * LANES)
        qat_ref[sl, :] = (head_norm_rope(pq[:, sl], gq) * Q_SCALE).T.astype(BF16)

    pk = jnp.dot(n, w_ref[:, o_ka:o_va], preferred_element_type=F32)
    pv = jnp.dot(n, w_ref[:, o_va:o_qb], preferred_element_type=F32)
    stash(jnp.dot(n, w_ref[:, o_kb:o_vb], preferred_element_type=F32), sk_ref, k1_ref)
    kr = head_norm_rope(pk, gk_ref[...])
    swapped = pltpu.roll(kr, HEAD_DIM, 1)
    low = lane < HEAD_DIM
    ka_ref[:, 0:LANES] = jnp.where(low, kr, swapped).astype(BF16)
    ka_ref[:, LANES:2 * LANES] = jnp.where(low, swapped, kr).astype(BF16)
    vat_ref[...] = pv.T.astype(BF16)
    write_phases(sq_ref, (q4_ref, q16_ref))

    stash(jnp.dot(n, w_ref[:, o_vb:], preferred_element_type=F32), sv_ref, v1_ref)
    write_phases(sk_ref, (k4_ref, k16_ref))
    write_phases(sv_ref, (v4_ref, v16_ref))


def _rope_tables(S):
    rows = S // GRID_W
    row = jnp.repeat(jnp.arange(rows, dtype=F32), GRID_W)
    col = jnp.tile(jnp.arange(GRID_W, dtype=F32), rows)
    n_freq = HEAD_DIM // 4
    inv = ROPE_THETA ** (-jnp.arange(n_freq, dtype=F32) / n_freq)
    ang_r = row[:, None] * inv[None, :]
    ang_c = col[:, None] * inv[None, :]
    cos_h = jnp.concatenate([jnp.cos(ang_r)] * 2 + [jnp.cos(ang_c)] * 2, axis=1)
    sin_h = jnp.concatenate([-jnp.sin(ang_r), jnp.sin(ang_r), -jnp.sin(ang_c), jnp.sin(ang_c)], axis=1)
    return jnp.tile(cos_h, (1, 2)), jnp.tile(sin_h, (1, 2))


def _in_projection(x, g, w_bf16, gq, gk):
    B, S, D = x.shape
    in_width = w_bf16.shape[1]
    cos, sin = _rope_tables(S)
    r = np.arange(2 * LANES)[:, None]
    c = np.arange(LANES)[None, :]
    ones2 = jnp.asarray(((r % LANES) // HEAD_DIM) == (c // HEAD_DIM), dtype=BF16)
    nt = S // TM_IN
    tok = lambda t, b: (b, t, 0)
    phase = lambda t, b: (b, 0, t, 0)
    const = lambda t, b: (0, 0)
    orders_shape, orders_spec = [], []
    for _ in range(3):
        orders_shape.append(jax.ShapeDtypeStruct((B, S, B_WIDTH), BF16))
        orders_spec.append(pl.BlockSpec((None, TM_IN, B_WIDTH), tok))
        for _, rr in DIL_BRANCHES[1:]:
            orders_shape.append(jax.ShapeDtypeStruct((B, rr, S // rr, B_WIDTH), BF16))
            orders_spec.append(pl.BlockSpec((None, rr, TM_IN // rr, B_WIDTH), phase))
    out_shape = (
        jax.ShapeDtypeStruct((B, A_WIDTH, S), BF16),
        jax.ShapeDtypeStruct((B, S, 2 * A_KV_WIDTH), BF16),
        jax.ShapeDtypeStruct((B, A_KV_WIDTH, S), BF16),
        *orders_shape,
    )
    return pl.pallas_call(
        _inproj_kernel,
        grid=(nt, B),
        in_specs=[
            pl.BlockSpec((None, TM_IN, D), tok),
            pl.BlockSpec((1, D), const),
            pl.BlockSpec((D, in_width), const),
            pl.BlockSpec((1, LANES), const),
            pl.BlockSpec((1, LANES), const),
            pl.BlockSpec((TM_IN, LANES), lambda t, b: (t, 0)),
            pl.BlockSpec((TM_IN, LANES), lambda t, b: (t, 0)),
            pl.BlockSpec((2 * LANES, LANES), const),
        ],
        out_specs=(
            pl.BlockSpec((None, A_WIDTH, TM_IN), lambda t, b: (b, 0, t)),
            pl.BlockSpec((None, TM_IN, 2 * A_KV_WIDTH), tok),
            pl.BlockSpec((None, A_KV_WIDTH, TM_IN), lambda t, b: (b, 0, t)),
            *orders_spec,
        ),
        out_shape=out_shape,
        scratch_shapes=[pltpu.VMEM((B_WIDTH // LANES, TM_IN, LANES), F32)] * 3,
        compiler_params=pltpu.CompilerParams(
            dimension_semantics=("arbitrary", "arbitrary"), vmem_limit_bytes=VMEM_LIMIT),
        name="in_projection",
    )(x, g, w_bf16, gq, gk, cos, sin, ones2)


def _attn_a_kernel(qt_ref, k_ref, vt0_ref, vt1_ref, g_ref, o_ref, s0_ref, s1_ref, m1_ref, half_ref):
    group = A_Q_HEADS // A_KV_HEADS
    width = group * TQ_A
    S = k_ref.shape[0]
    top = lax.broadcasted_iota(jnp.int32, (LANES, TQ_A), 0) < HEAD_DIM
    ones = jnp.ones((BF16_ROWS, KEY_CHUNK), BF16)

    @pl.when(pl.program_id(0) == 0)
    def _():
        s1_ref[...] = jnp.zeros_like(s1_ref)
        m1_ref[...] = jnp.zeros_like(m1_ref)
        half_ref[...] = jnp.zeros_like(half_ref)

    def half_step(kv, s_new_ref, s_old_ref, old_max, vt_old_ref, side_work=None):
        cols = []
        for j in range(group // 2):
            r0 = (kv * group + 2 * j) * HEAD_DIM
            qt2 = qt_ref[r0:r0 + LANES, :]
            zero = jnp.zeros_like(qt2)
            cols += [jnp.where(top, qt2, zero), jnp.where(top, zero, qt2)]
        w = jnp.concatenate(cols, axis=1)
        run_max = jnp.full((SUBLANES, width), -jnp.inf, F32)
        acc = jnp.zeros((HEAD_DIM + BF16_ROWS, width), F32)
        for c in range(S // KEY_CHUNK):
            rows = slice(c * KEY_CHUNK, (c + 1) * KEY_CHUNK)
            st = jnp.dot(k_ref[rows, kv * LANES:(kv + 1) * LANES], w,
                         preferred_element_type=F32)
            p = jnp.exp2(s_old_ref[rows, :] - old_max).astype(BF16)
            vaug = jnp.concatenate([vt_old_ref[:, rows], ones], axis=0)
            acc = acc + jnp.dot(vaug, p, preferred_element_type=F32)
            s_new_ref[rows, :] = st
            run_max = jnp.maximum(run_max, st.reshape(KEY_CHUNK // SUBLANES, SUBLANES, width).max(axis=0))
            if side_work is not None and c == 0:
                side_work()
        ot = acc[0:HEAD_DIM] * (1.0 / acc[HEAD_DIM:HEAD_DIM + 1])
        return jnp.max(run_max, axis=0, keepdims=True), ot

    def head_blocks(ot):
        return [jnp.concatenate([ot[:, 2 * j * TQ_A:(2 * j + 1) * TQ_A],
                                 ot[:, (2 * j + 1) * TQ_A:(2 * j + 2) * TQ_A]], axis=0).T
                for j in range(group // 2)]

    m0, late = half_step(0, s0_ref, s1_ref, m1_ref[...], vt1_ref)

    def write_previous_block():
        y = jnp.concatenate(head_blocks(half_ref[...]) + head_blocks(late), axis=1)
        o_ref[...] = _rms(y, g_ref[...]).astype(BF16)

    m1, early = half_step(1, s1_ref, s0_ref, m0, vt0_ref, write_previous_block)
    m1_ref[...] = m1
    half_ref[...] = early


def _attention_a(qat, ka, vat, g):
    B, _, S = qat.shape
    assert A_KV_HEADS == 2, "the step skew pairs exactly two kv groups"
    width = (A_Q_HEADS // A_KV_HEADS) * TQ_A
    nq = S // TQ_A
    n_steps = B * nq
    cur = lambda t: jnp.minimum(t, n_steps - 1)
    prev = lambda t: jnp.maximum(t - 1, 0)
    return pl.pallas_call(
        _attn_a_kernel,
        grid=(n_steps + 1,),
        in_specs=[
            pl.BlockSpec((None, A_WIDTH, TQ_A), lambda t: (cur(t) // nq, 0, cur(t) % nq)),
            pl.BlockSpec((None, S, 2 * A_KV_WIDTH), lambda t: (cur(t) // nq, 0, 0)),
            pl.BlockSpec((None, HEAD_DIM, S), lambda t: (cur(t) // nq, 0, 0)),
            pl.BlockSpec((None, HEAD_DIM, S), lambda t: (prev(t) // nq, 1, 0)),
            pl.BlockSpec((1, A_WIDTH), lambda t: (0, 0)),
        ],
        out_specs=pl.BlockSpec((None, TQ_A, A_WIDTH), lambda t: (prev(t) // nq, prev(t) % nq, 0)),
        out_shape=jax.ShapeDtypeStruct((B, S, A_WIDTH), BF16),
        scratch_shapes=[pltpu.VMEM((S, width), F32), pltpu.VMEM((S, width), F32),
                        pltpu.VMEM((1, width), F32), pltpu.VMEM((HEAD_DIM, width), F32)],
        compiler_params=pltpu.CompilerParams(
            dimension_semantics=("arbitrary",), vmem_limit_bytes=VMEM_LIMIT),
        name="attention_a",
    )(qat, ka, vat, vat, g)


def _t5_bucket(rel):
    rel = np.asarray(rel, np.int32)
    nb_half = N_BUCKETS // 2
    max_exact = nb_half // 2
    ret = np.where(rel > 0, nb_half, 0)
    n = np.abs(rel)
    large = max_exact + (np.log(np.maximum(n, 1).astype(np.float32) / max_exact)
                         / math.log(BUCKET_MAX_DIST / max_exact)
                         * (nb_half - max_exact)).astype(np.int32)
    large = np.minimum(large, nb_half - 1)
    return (ret + np.where(n < max_exact, n, large)).astype(np.int32)


def _variant_offset(v):
    return (0, -HALF_WINDOW, -2 * HALF_WINDOW)[v]


def _bias_kernel(rbt_ref, code_ref, valid_ref, o_ref):
    h = pl.program_id(0)
    rb = rbt_ref[pl.ds(h, 1), :]
    for n in range(len(DIL_BRANCHES)):
        code = code_ref[n:n + 1, :]
        acc = jnp.zeros(code.shape, F32)
        for b in range(N_BUCKETS):
            acc = jnp.where(code == b, rb[:, b:b + 1], acc)
        f = jnp.where(valid_ref[n:n + 1, :] > 0, acc * LOG2E, NEG)
        rows = jnp.broadcast_to(f, (TQ, BIAS_SPAN))
        for v in range(N_VARIANTS):
            shift = (KEY_WINDOW - _variant_offset(v)) % BIAS_SPAN
            o_ref[n * N_VARIANTS + v] = pltpu.roll(rows, shift, 1, stride=1, stride_axis=0)[:, 0:KEY_WINDOW]


def _bias_blocks(rel_bias):
    rel = np.arange(BIAS_SPAN, dtype=np.int32) - KEY_WINDOW
    code = np.stack([_t5_bucket(rel * r) for _, r in DIL_BRANCHES])
    valid = np.broadcast_to((np.abs(rel) <= HALF_WINDOW).astype(np.int32), code.shape)
    n_blocks = len(DIL_BRANCHES) * N_VARIANTS
    const = lambda h: (0, 0)
    return pl.pallas_call(
        _bias_kernel,
        grid=(B_HEADS,),
        in_specs=[pl.BlockSpec((B_HEADS, N_BUCKETS), const),
                  pl.BlockSpec(code.shape, const),
                  pl.BlockSpec(code.shape, const)],
        out_specs=pl.BlockSpec((n_blocks, None, TQ, KEY_WINDOW), lambda h: (0, h, 0, 0)),
        out_shape=jax.ShapeDtypeStruct((n_blocks, B_HEADS, TQ, KEY_WINDOW), F32),
        compiler_params=pltpu.CompilerParams(dimension_semantics=("arbitrary",)),
        name="bias_blocks",
    )(rel_bias.T, jnp.asarray(code), jnp.asarray(valid))


def _split_pair(q2):
    lane = lax.broadcasted_iota(jnp.int32, q2.shape, 1)
    low = lane < HEAD_DIM
    zero = jnp.zeros_like(q2)
    return jnp.concatenate([jnp.where(low, q2, zero), jnp.where(low, zero, q2)], axis=0)


def _attn_b_kernel(q1_ref, q4_ref, q16_ref, k1_ref, k4_ref, k16_ref, v1_ref, v4_ref, v16_ref,
                   bias_ref, g_ref, o_ref, y_ref, *acc_refs):
    pair = pl.program_id(1)
    S = q1_ref.shape[0]
    n_blocks = S // TQ
    low = lax.broadcasted_iota(jnp.int32, (TQ, LANES), 1) < HEAD_DIM
    q_refs = (q1_ref, q4_ref, q16_ref)
    k_refs = (k1_ref, k4_ref, k16_ref)
    v_refs = (v1_ref, v4_ref, v16_ref)

    def merge(x2):
        return jnp.where(low, x2[0:TQ], x2[TQ:2 * TQ])

    def banded_block(q2, k2, v2, bias):
        s = lax.dot_general(_split_pair(q2), k2, (((1,), (1,)), ((), ())),
                            preferred_element_type=F32) + bias
        m = jnp.max(s, axis=1, keepdims=True)
        p = jnp.exp2(s - m).astype(BF16)
        vaug = jnp.concatenate([v2, jnp.ones_like(v2)], axis=1)
        oa = jnp.dot(p, vaug, preferred_element_type=F32)
        return merge(oa[:, 0:LANES]), merge(jnp.broadcast_to(m, (2 * TQ, LANES))), merge(oa[:, LANES:2 * LANES])

    def branch_block(n, i):
        r = DIL_BRANCHES[n][1]
        length = S // r
        per_phase = length // TQ
        if per_phase > 1:
            ph = i // per_phase
            blk = i - ph * per_phase
            variant = jnp.where(blk == 0, 0, jnp.where(blk == per_phase - 1, 2, 1))
            r0 = pl.multiple_of(blk * TQ, TQ)
            w0 = pl.multiple_of(jnp.clip(blk * TQ - HALF_WINDOW, 0, length - KEY_WINDOW), HALF_WINDOW)
            width = KEY_WINDOW
        else:
            ph, variant, r0, w0, width = i, 0, 0, 0, length
        tab = n * N_VARIANTS + variant
        bias = jnp.concatenate([bias_ref[tab, 2 * pair, :, 0:width],
                                bias_ref[tab, 2 * pair + 1, :, 0:width]], axis=0)
        if r == 1:
            q2 = q_refs[n][pl.ds(r0, TQ), :]
            k2 = k_refs[n][pl.ds(w0, width), :]
            v2 = v_refs[n][pl.ds(w0, width), :]
        else:
            q2 = q_refs[n][ph, pl.ds(r0, TQ), :]
            k2 = k_refs[n][ph, pl.ds(w0, width), :]
            v2 = v_refs[n][ph, pl.ds(w0, width), :]
        rows = pl.ds(r0 * r + ph, TQ, stride=r) if r > 1 else pl.ds(r0, TQ)
        return banded_block(q2, k2, v2, bias), rows

    def dilated_step(i, carry):
        for n in range(1, len(DIL_BRANCHES)):
            parts, rows = branch_block(n, i)
            for j, part in enumerate(parts):
                acc_refs[3 * (n - 1) + j][rows, :] = part
        return carry

    def dense_step(i, carry):
        parts, rows = branch_block(0, i)
        branches = [parts] + [tuple(acc_refs[3 * (n - 1) + j][rows, :] for j in range(3))
                              for n in range(1, len(DIL_BRANCHES))]
        top = functools.reduce(jnp.maximum, [m for _, m, _ in branches])
        scales = [jnp.exp2(m - top) for _, m, _ in branches]
        num = sum(w * o for w, (o, _, _) in zip(scales, branches))
        den = sum(w * d for w, (_, _, d) in zip(scales, branches))
        y_ref[pair, rows, :] = num * (1.0 / den)
        return carry

    lax.fori_loop(0, n_blocks, dilated_step, 0, unroll=B_UNROLL)
    lax.fori_loop(0, n_blocks, dense_step, 0, unroll=B_UNROLL)

    @pl.when(pair == B_HEADS // 2 - 1)
    def _():
        y = jnp.concatenate([y_ref[c] for c in range(B_HEADS // 2)], axis=1)
        o_ref[...] = _rms(y, g_ref[...]).astype(BF16)


def _attention_b(orders, bias, g):
    B, S, _ = orders[0].shape
    specs = []
    for a in orders:
        if a.ndim == 3:
            specs.append(pl.BlockSpec((None, S, LANES), lambda b, j: (b, 0, j)))
        else:
            specs.append(pl.BlockSpec((None, a.shape[1], a.shape[2], LANES), lambda b, j: (b, 0, 0, j)))
    return pl.pallas_call(
        _attn_b_kernel,
        grid=(B, B_HEADS // 2),
        in_specs=specs + [
            pl.BlockSpec(bias.shape, lambda b, j: (0, 0, 0, 0), pipeline_mode=pl.Buffered(1)),
            pl.BlockSpec((1, B_WIDTH), lambda b, j: (0, 0)),
        ],
        out_specs=pl.BlockSpec((None, S, B_WIDTH), lambda b, j: (b, 0, 0)),
        out_shape=jax.ShapeDtypeStruct((B, S, B_WIDTH), BF16),
        scratch_shapes=[pltpu.VMEM((B_HEADS // 2, S, LANES), F32)] + [pltpu.VMEM((S, LANES), F32)] * 6,
        compiler_params=pltpu.CompilerParams(
            dimension_semantics=("arbitrary", "arbitrary"), vmem_limit_bytes=VMEM_LIMIT),
        name="attention_b",
    )(*orders, bias, g)


def _out_mlp_kernel(x_ref, ma_ref, mb_ref, wo_ref, gm_ref, wup_ref, wdn_ref, gf_ref, o_ref):
    mix = jnp.concatenate([ma_ref[...], mb_ref[...]], axis=1)
    h = x_ref[...] + jnp.dot(mix, wo_ref[...], preferred_element_type=F32)
    hn = _rms(h, gm_ref[...]).astype(BF16)
    acc = h
    d_ff = wup_ref.shape[1]
    for c in range(d_ff // FF_CHUNK):
        sl = slice(c * FF_CHUNK, (c + 1) * FF_CHUNK)
        u = jnp.maximum(jnp.dot(hn, wup_ref[:, sl], preferred_element_type=F32), 0.0)
        acc = acc + jnp.dot((u * u).astype(BF16), wdn_ref[sl, :], preferred_element_type=F32)
    o_ref[...] = _rms(acc, gf_ref[...])


def _out_mlp(x2, ma, mb, wo, gm, wup, wdn, gf):
    N, D = x2.shape
    d_ff = wup.shape[1]
    tok = lambda t: (t, 0)
    const = lambda t: (0, 0)
    resident = functools.partial(pl.BlockSpec, index_map=const, pipeline_mode=pl.Buffered(1))
    return pl.pallas_call(
        _out_mlp_kernel,
        grid=(N // TM_OUT,),
        in_specs=[
            pl.BlockSpec((TM_OUT, D), tok),
            pl.BlockSpec((TM_OUT, A_WIDTH), tok),
            pl.BlockSpec((TM_OUT, B_WIDTH), tok),
            resident((A_WIDTH + B_WIDTH, D)),
            pl.BlockSpec((1, D), const),
            resident((D, d_ff)),
            resident((d_ff, D)),
            pl.BlockSpec((1, D), const),
        ],
        out_specs=pl.BlockSpec((TM_OUT, D), tok),
        out_shape=jax.ShapeDtypeStruct((N, D), F32),
        compiler_params=pltpu.CompilerParams(
            dimension_semantics=("arbitrary",), vmem_limit_bytes=VMEM_LIMIT),
        name="out_mlp",
    )(x2, ma, mb, wo, gm, wup, wdn, gf)


def kernel(x, attn_norm_g, w_in, q_norm_g, k_norm_g, rel_bias, out_norm_a_g, out_norm_b_g,
           w_out, mlp_norm_g, w_up, w_down, final_norm_g):
    B, S, D = x.shape
    assert w_in.shape[0] == 1, "the output kernel fuses the final norm: single layer only"
    bias = _bias_blocks(rel_bias)
    gq = jnp.tile(q_norm_g[0], LANES // HEAD_DIM).reshape(1, LANES)
    gk = jnp.tile(k_norm_g[0], LANES // HEAD_DIM).reshape(1, LANES)
    qat, ka, vat, *orders = _in_projection(
        x, attn_norm_g[0].reshape(1, D), w_in[0].astype(BF16), gq, gk)
    ma = _attention_a(qat, ka, vat, out_norm_a_g[0].reshape(1, A_WIDTH))
    mb = _attention_b(orders, bias, out_norm_b_g[0].reshape(1, B_WIDTH))
    out = _out_mlp(x.reshape(B * S, D), ma.reshape(B * S, A_WIDTH), mb.reshape(B * S, B_WIDTH),
                   w_out[0].astype(BF16), mlp_norm_g[0].reshape(1, D),
                   w_up[0].astype(BF16), w_down[0].astype(BF16), final_norm_g.reshape(1, D))
    return out.reshape(B, S, D)
```

```python
import functools
import math

import jax
import jax.numpy as jnp
import numpy as np
from jax import lax
from jax.experimental import pallas as pl
from jax.experimental.pallas import tpu as pltpu

HEAD_DIM = 64
A_Q_HEADS = 8
A_KV_HEADS = 2
B_HEADS = 8
GRID_W = 64
ROPE_THETA = 10000.0
DIL_BRANCHES = ((128, 1), (512, 4), (2048, 16))
N_BUCKETS = 32
BUCKET_MAX_DIST = 1024
EPS = 1e-6
NEG = -1e30

A_WIDTH = A_Q_HEADS * HEAD_DIM
A_KV_WIDTH = A_KV_HEADS * HEAD_DIM
B_WIDTH = B_HEADS * HEAD_DIM
LOG2E = math.log2(math.e)
Q_SCALE = HEAD_DIM ** -0.5 * LOG2E

LANES = 128
SUBLANES = 8
BF16_ROWS = 16
ROT = HEAD_DIM // 4
TM_IN = 512
TQ = 128
TQ_A = 256
KEY_CHUNK = 256
TM_OUT = 1024
FF_CHUNK = 1024
VMEM_LIMIT = 56 * 1024 * 1024

HALF_WINDOW = DIL_BRANCHES[0][0] // (2 * DIL_BRANCHES[0][1])
assert all(w // (2 * r) == HALF_WINDOW for w, r in DIL_BRANCHES)
KEY_WINDOW = TQ + 2 * HALF_WINDOW
BIAS_SPAN = 2 * KEY_WINDOW
N_VARIANTS = 3
B_UNROLL = 16

F32 = jnp.float32
BF16 = jnp.bfloat16


def _rms(x, g):
    return x * lax.rsqrt(jnp.mean(x * x, axis=-1, keepdims=True) + EPS) * g


def _inproj_kernel(x_ref, g_ref, w_ref, gq_ref, gk_ref, cos_ref, sin_ref, ones_ref,
                   qat_ref, ka_ref, vat_ref,
                   q1_ref, q4_ref, q16_ref, k1_ref, k4_ref, k16_ref, v1_ref, v4_ref, v16_ref,
                   sq_ref, sk_ref, sv_ref):
    n = _rms(x_ref[...], g_ref[...]).astype(BF16)
    cos = cos_ref[...]
    sin = sin_ref[...]
    ones2 = ones_ref[...]
    lane = lax.broadcasted_iota(jnp.int32, (TM_IN, LANES), 1)
    first_half = (lane % (2 * ROT)) < ROT

    def head_norm_rope(blk, gain):
        sq = blk * blk
        hi = sq.astype(BF16)
        lo = (sq - hi.astype(F32)).astype(BF16)
        ss = jnp.dot(jnp.concatenate([hi, lo], axis=1), ones2, preferred_element_type=F32)
        xn = blk * lax.rsqrt(ss * (1.0 / HEAD_DIM) + EPS) * gain
        partner = jnp.where(first_half, pltpu.roll(xn, LANES - ROT, 1), pltpu.roll(xn, ROT, 1))
        return xn * cos + partner * sin

    o_ka = A_WIDTH
    o_va = o_ka + A_KV_WIDTH
    o_qb = o_va + A_KV_WIDTH
    o_kb = o_qb + B_WIDTH
    o_vb = o_kb + B_WIDTH

    def stash(val, scr_ref, nat_ref):
        nat_ref[...] = val.astype(BF16)
        for c in range(B_WIDTH // LANES):
            scr_ref[c] = val[:, c * LANES:(c + 1) * LANES]

    def write_phases(scr_ref, ph_refs):
        for c in range(B_WIDTH // LANES):
            sl = slice(c * LANES, (c + 1) * LANES)
            for (_, r), ph_ref in zip(DIL_BRANCHES[1:], ph_refs):
                for p in range(r):
                    ph_ref[p, :, sl] = scr_ref[c, pl.ds(p, TM_IN // r, stride=r), :].astype(BF16)

    pq = jnp.dot(n, w_ref[:, 0:o_ka], preferred_element_type=F32)
    stash(jnp.dot(n, w_ref[:, o_qb:o_kb], preferred_element_type=F32) * Q_SCALE, sq_ref, q1_ref)
    gq = gq_ref[...]
    for j in range(A_WIDTH // LANES):
        sl = slice(j * LANES, (j + 1) * LANES)
        qat_ref[sl, :] = (head_norm_rope(pq[:, sl], gq) * Q_SCALE).T.astype(BF16)

    pk = jnp.dot(n, w_ref[:, o_ka:o_va], preferred_element_type=F32)
    pv = jnp.dot(n, w_ref[:, o_va:o_qb], preferred_element_type=F32)
    stash(jnp.dot(n, w_ref[:, o_kb:o_vb], preferred_element_type=F32), sk_ref, k1_ref)
    kr = head_norm_rope(pk, gk_ref[...])
    swapped = pltpu.roll(kr, HEAD_DIM, 1)
    low = lane < HEAD_DIM
    ka_ref[:, 0:LANES] = jnp.where(low, kr, swapped).astype(BF16)
    ka_ref[:, LANES:2 * LANES] = jnp.where(low, swapped, kr).astype(BF16)
    vat_ref[...] = pv.T.astype(BF16)
    write_phases(sq_ref, (q4_ref, q16_ref))

    stash(jnp.dot(n, w_ref[:, o_vb:], preferred_element_type=F32), sv_ref, v1_ref)
    write_phases(sk_ref, (k4_ref, k16_ref))
    write_phases(sv_ref, (v4_ref, v16_ref))


def _rope_tables(S):
    rows = S // GRID_W
    row = jnp.repeat(jnp.arange(rows, dtype=F32), GRID_W)
    col = jnp.tile(jnp.arange(GRID_W, dtype=F32), rows)
    n_freq = HEAD_DIM // 4
    inv = ROPE_THETA ** (-jnp.arange(n_freq, dtype=F32) / n_freq)
    ang_r = row[:, None] * inv[None, :]
    ang_c = col[:, None] * inv[None, :]
    cos_h = jnp.concatenate([jnp.cos(ang_r)] * 2 + [jnp.cos(ang_c)] * 2, axis=1)
    sin_h = jnp.concatenate([-jnp.sin(ang_r), jnp.sin(ang_r), -jnp.sin(ang_c), jnp.sin(ang_c)], axis=1)
    return jnp.tile(cos_h, (1, 2)), jnp.tile(sin_h, (1, 2))


def _in_projection(x, g, w_bf16, gq, gk):
    B, S, D = x.shape
    in_width = w_bf16.shape[1]
    cos, sin = _rope_tables(S)
    r = np.arange(2 * LANES)[:, None]
    c = np.arange(LANES)[None, :]
    ones2 = jnp.asarray(((r % LANES) // HEAD_DIM) == (c // HEAD_DIM), dtype=BF16)
    nt = S // TM_IN
    tok = lambda t, b: (b, t, 0)
    phase = lambda t, b: (b, 0, t, 0)
    const = lambda t, b: (0, 0)
    orders_shape, orders_spec = [], []
    for _ in range(3):
        orders_shape.append(jax.ShapeDtypeStruct((B, S, B_WIDTH), BF16))
        orders_spec.append(pl.BlockSpec((None, TM_IN, B_WIDTH), tok))
        for _, rr in DIL_BRANCHES[1:]:
            orders_shape.append(jax.ShapeDtypeStruct((B, rr, S // rr, B_WIDTH), BF16))
            orders_spec.append(pl.BlockSpec((None, rr, TM_IN // rr, B_WIDTH), phase))
    out_shape = (
        jax.ShapeDtypeStruct((B, A_WIDTH, S), BF16),
        jax.ShapeDtypeStruct((B, S, 2 * A_KV_WIDTH), BF16),
        jax.ShapeDtypeStruct((B, A_KV_WIDTH, S), BF16),
        *orders_shape,
    )
    return pl.pallas_call(
        _inproj_kernel,
        grid=(nt, B),
        in_specs=[
            pl.BlockSpec((None, TM_IN, D), tok),
            pl.BlockSpec((1, D), const),
            pl.BlockSpec((D, in_width), const),
            pl.BlockSpec((1, LANES), const),
            pl.BlockSpec((1, LANES), const),
            pl.BlockSpec((TM_IN, LANES), lambda t, b: (t, 0)),
            pl.BlockSpec((TM_IN, LANES), lambda t, b: (t, 0)),
            pl.BlockSpec((2 * LANES, LANES), const),
        ],
        out_specs=(
            pl.BlockSpec((None, A_WIDTH, TM_IN), lambda t, b: (b, 0, t)),
            pl.BlockSpec((None, TM_IN, 2 * A_KV_WIDTH), tok),
            pl.BlockSpec((None, A_KV_WIDTH, TM_IN), lambda t, b: (b, 0, t)),
            *orders_spec,
        ),
        out_shape=out_shape,
        scratch_shapes=[pltpu.VMEM((B_WIDTH // LANES, TM_IN, LANES), F32)] * 3,
        compiler_params=pltpu.CompilerParams(
            dimension_semantics=("arbitrary", "arbitrary"), vmem_limit_bytes=VMEM_LIMIT),
        name="in_projection",
    )(x, g, w_bf16, gq, gk, cos, sin, ones2)


def _attn_a_kernel(qt_ref, k_ref, vt0_ref, vt1_ref, g_ref, o_ref, s0_ref, s1_ref, m1_ref, half_ref):
    group = A_Q_HEADS // A_KV_HEADS
    width = group * TQ_A
    S = k_ref.shape[0]
    top = lax.broadcasted_iota(jnp.int32, (LANES, TQ_A), 0) < HEAD_DIM
    ones = jnp.ones((BF16_ROWS, KEY_CHUNK), BF16)

    @pl.when(pl.program_id(0) == 0)
    def _():
        s1_ref[...] = jnp.zeros_like(s1_ref)
        m1_ref[...] = jnp.zeros_like(m1_ref)
        half_ref[...] = jnp.zeros_like(half_ref)

    def half_step(kv, s_new_ref, s_old_ref, old_max, vt_old_ref):
        cols = []
        for j in range(group // 2):
            r0 = (kv * group + 2 * j) * HEAD_DIM
            qt2 = qt_ref[r0:r0 + LANES, :]
            zero = jnp.zeros_like(qt2)
            cols += [jnp.where(top, qt2, zero), jnp.where(top, zero, qt2)]
        w = jnp.concatenate(cols, axis=1)
        run_max = jnp.full((SUBLANES, width), -jnp.inf, F32)
        acc = jnp.zeros((HEAD_DIM + BF16_ROWS, width), F32)
        for c in range(S // KEY_CHUNK):
            rows = slice(c * KEY_CHUNK, (c + 1) * KEY_CHUNK)
            st = jnp.dot(k_ref[rows, kv * LANES:(kv + 1) * LANES], w,
                         preferred_element_type=F32)
            p = jnp.exp2(s_old_ref[rows, :] - old_max).astype(BF16)
            vaug = jnp.concatenate([vt_old_ref[:, rows], ones], axis=0)
            acc = acc + jnp.dot(vaug, p, preferred_element_type=F32)
            s_new_ref[rows, :] = st
            run_max = jnp.maximum(run_max, st.reshape(KEY_CHUNK // SUBLANES, SUBLANES, width).max(axis=0))
        ot = acc[0:HEAD_DIM] * (1.0 / acc[HEAD_DIM:HEAD_DIM + 1])
        return jnp.max(run_max, axis=0, keepdims=True), ot

    def head_blocks(ot):
        return [jnp.concatenate([ot[:, 2 * j * TQ_A:(2 * j + 1) * TQ_A],
                                 ot[:, (2 * j + 1) * TQ_A:(2 * j + 2) * TQ_A]], axis=0).T
                for j in range(group // 2)]

    m0, late = half_step(0, s0_ref, s1_ref, m1_ref[...], vt1_ref)

    y = jnp.concatenate(head_blocks(half_ref[...]) + head_blocks(late), axis=1)
    o_ref[...] = _rms(y, g_ref[...]).astype(BF16)
    m1, early = half_step(1, s1_ref, s0_ref, m0, vt0_ref)
    m1_ref[...] = m1
    half_ref[...] = early


def _attention_a(qat, ka, vat, g):
    B, _, S = qat.shape
    assert A_KV_HEADS == 2, "the step skew pairs exactly two kv groups"
    width = (A_Q_HEADS // A_KV_HEADS) * TQ_A
    nq = S // TQ_A
    n_steps = B * nq
    cur = lambda t: jnp.minimum(t, n_steps - 1)
    prev = lambda t: jnp.maximum(t - 1, 0)
    return pl.pallas_call(
        _attn_a_kernel,
        grid=(n_steps + 1,),
        in_specs=[
            pl.BlockSpec((None, A_WIDTH, TQ_A), lambda t: (cur(t) // nq, 0, cur(t) % nq)),
            pl.BlockSpec((None, S, 2 * A_KV_WIDTH), lambda t: (cur(t) // nq, 0, 0)),
            pl.BlockSpec((None, HEAD_DIM, S), lambda t: (cur(t) // nq, 0, 0)),
            pl.BlockSpec((None, HEAD_DIM, S), lambda t: (prev(t) // nq, 1, 0)),
            pl.BlockSpec((1, A_WIDTH), lambda t: (0, 0)),
        ],
        out_specs=pl.BlockSpec((None, TQ_A, A_WIDTH), lambda t: (prev(t) // nq, prev(t) % nq, 0)),
        out_shape=jax.ShapeDtypeStruct((B, S, A_WIDTH), BF16),
        scratch_shapes=[pltpu.VMEM((S, width), F32), pltpu.VMEM((S, width), F32),
                        pltpu.VMEM((1, width), F32), pltpu.VMEM((HEAD_DIM, width), F32)],
        compiler_params=pltpu.CompilerParams(
            dimension_semantics=("arbitrary",), vmem_limit_bytes=VMEM_LIMIT),
        name="attention_a",
    )(qat, ka, vat, vat, g)


def _t5_bucket(rel):
    rel = np.asarray(rel, np.int32)
    nb_half = N_BUCKETS // 2
    max_exact = nb_half // 2
    ret = np.where(rel > 0, nb_half, 0)
    n = np.abs(rel)
    large = max_exact + (np.log(np.maximum(n, 1).astype(np.float32) / max_exact)
                         / math.log(BUCKET_MAX_DIST / max_exact)
                         * (nb_half - max_exact)).astype(np.int32)
    large = np.minimum(large, nb_half - 1)
    return (ret + np.where(n < max_exact, n, large)).astype(np.int32)


def _variant_offset(v):
    return (0, -HALF_WINDOW, -2 * HALF_WINDOW)[v]


def _bias_kernel(rbt_ref, code_ref, valid_ref, o_ref):
    h = pl.program_id(0)
    rb = rbt_ref[pl.ds(h, 1), :]
    for n in range(len(DIL_BRANCHES)):
        code = code_ref[n:n + 1, :]
        acc = jnp.zeros(code.shape, F32)
        for b in range(N_BUCKETS):
            acc = jnp.where(code == b, rb[:, b:b + 1], acc)
        f = jnp.where(valid_ref[n:n + 1, :] > 0, acc * LOG2E, NEG)
        rows = jnp.broadcast_to(f, (TQ, BIAS_SPAN))
        for v in range(N_VARIANTS):
            shift = (KEY_WINDOW - _variant_offset(v)) % BIAS_SPAN
            o_ref[n * N_VARIANTS + v] = pltpu.roll(rows, shift, 1, stride=1, stride_axis=0)[:, 0:KEY_WINDOW]


def _bias_blocks(rel_bias):
    rel = np.arange(BIAS_SPAN, dtype=np.int32) - KEY_WINDOW
    code = np.stack([_t5_bucket(rel * r) for _, r in DIL_BRANCHES])
    valid = np.broadcast_to((np.abs(rel) <= HALF_WINDOW).astype(np.int32), code.shape)
    n_blocks = len(DIL_BRANCHES) * N_VARIANTS
    const = lambda h: (0, 0)
    return pl.pallas_call(
        _bias_kernel,
        grid=(B_HEADS,),
        in_specs=[pl.BlockSpec((B_HEADS, N_BUCKETS), const),
                  pl.BlockSpec(code.shape, const),
                  pl.BlockSpec(code.shape, const)],
        out_specs=pl.BlockSpec((n_blocks, None, TQ, KEY_WINDOW), lambda h: (0, h, 0, 0)),
        out_shape=jax.ShapeDtypeStruct((n_blocks, B_HEADS, TQ, KEY_WINDOW), F32),
        compiler_params=pltpu.CompilerParams(dimension_semantics=("arbitrary",)),
        name="bias_blocks",
    )(rel_bias.T, jnp.asarray(code), jnp.asarray(valid))


def _split_pair(q2):
    lane = lax.broadcasted_iota(jnp.int32, q2.shape, 1)
    low = lane < HEAD_DIM
    zero = jnp.zeros_like(q2)
    return jnp.concatenate([jnp.where(low, q2, zero), jnp.where(low, zero, q2)], axis=0)


def _attn_b_kernel(q1_ref, q4_ref, q16_ref, k1_ref, k4_ref, k16_ref, v1_ref, v4_ref, v16_ref,
                   bias_ref, g_ref, o_ref, y_ref, *acc_refs):
    pair = pl.program_id(1)
    S = q1_ref.shape[0]
    n_blocks = S // TQ
    low = lax.broadcasted_iota(jnp.int32, (TQ, LANES), 1) < HEAD_DIM
    q_refs = (q1_ref, q4_ref, q16_ref)
    k_refs = (k1_ref, k4_ref, k16_ref)
    v_refs = (v1_ref, v4_ref, v16_ref)

    def merge(x2):
        return jnp.where(low, x2[0:TQ], x2[TQ:2 * TQ])

    def banded_block(q2, k2, v2, bias):
        s = lax.dot_general(_split_pair(q2), k2, (((1,), (1,)), ((), ())),
                            preferred_element_type=F32) + bias
        m = jnp.max(s, axis=1, keepdims=True)
        p = jnp.exp2(s - m).astype(BF16)
        vaug = jnp.concatenate([v2, jnp.ones_like(v2)], axis=1)
        oa = jnp.dot(p, vaug, preferred_element_type=F32)
        return merge(oa[:, 0:LANES]), merge(jnp.broadcast_to(m, (2 * TQ, LANES))), merge(oa[:, LANES:2 * LANES])

    def branch_block(n, i):
        r = DIL_BRANCHES[n][1]
        length = S // r
        per_phase = length // TQ
        if per_phase > 1:
            ph = i // per_phase
            blk = i - ph * per_phase
            variant = jnp.where(blk == 0, 0, jnp.where(blk == per_phase - 1, 2, 1))
            r0 = pl.multiple_of(blk * TQ, TQ)
            w0 = pl.multiple_of(jnp.clip(blk * TQ - HALF_WINDOW, 0, length - KEY_WINDOW), HALF_WINDOW)
            width = KEY_WINDOW
        else:
            ph, variant, r0, w0, width = i, 0, 0, 0, length
        tab = n * N_VARIANTS + variant
        bias = jnp.concatenate([bias_ref[tab, 2 * pair, :, 0:width],
                                bias_ref[tab, 2 * pair + 1, :, 0:width]], axis=0)
        if r == 1:
            q2 = q_refs[n][pl.ds(r0, TQ), :]
            k2 = k_refs[n][pl.ds(w0, width), :]
            v2 = v_refs[n][pl.ds(w0, width), :]
        else:
            q2 = q_refs[n][ph, pl.ds(r0, TQ), :]
            k2 = k_refs[n][ph, pl.ds(w0, width), :]
            v2 = v_refs[n][ph, pl.ds(w0, width), :]
        rows = pl.ds(r0 * r + ph, TQ, stride=r) if r > 1 else pl.ds(r0, TQ)
        return banded_block(q2, k2, v2, bias), rows

    def dilated_step(i, carry):
        for n in range(1, len(DIL_BRANCHES)):
            parts, rows = branch_block(n, i)
            for j, part in enumerate(parts):
                acc_refs[3 * (n - 1) + j][rows, :] = part
        return carry

    def dense_step(i, carry):
        parts, rows = branch_block(0, i)
        branches = [parts] + [tuple(acc_refs[3 * (n - 1) + j][rows, :] for j in range(3))
                              for n in range(1, len(DIL_BRANCHES))]
        top = functools.reduce(jnp.maximum, [m for _, m, _ in branches])
        scales = [jnp.exp2(m - top) for _, m, _ in branches]
        num = sum(w * o for w, (o, _, _) in zip(scales, branches))
        den = sum(w * d for w, (_, _, d) in zip(scales, branches))
        y_ref[pair, rows, :] = num * (1.0 / den)
        return carry

    lax.fori_loop(0, n_blocks, dilated_step, 0, unroll=B_UNROLL)
    lax.fori_loop(0, n_blocks, dense_step, 0, unroll=B_UNROLL)

    @pl.when(pair == B_HEADS // 2 - 1)
    def _():
        y = jnp.concatenate([y_ref[c] for c in range(B_HEADS // 2)], axis=1)
        o_ref[...] = _rms(y, g_ref[...]).astype(BF16)


def _attention_b(orders, bias, g):
    B, S, _ = orders[0].shape
    specs = []
    for a in orders:
        if a.ndim == 3:
            specs.append(pl.BlockSpec((None, S, LANES), lambda b, j: (b, 0, j)))
        else:
            specs.append(pl.BlockSpec((None, a.shape[1], a.shape[2], LANES), lambda b, j: (b, 0, 0, j)))
    return pl.pallas_call(
        _attn_b_kernel,
        grid=(B, B_HEADS // 2),
        in_specs=specs + [
            pl.BlockSpec(bias.shape, lambda b, j: (0, 0, 0, 0), pipeline_mode=pl.Buffered(1)),
            pl.BlockSpec((1, B_WIDTH), lambda b, j: (0, 0)),
        ],
        out_specs=pl.BlockSpec((None, S, B_WIDTH), lambda b, j: (b, 0, 0)),
        out_shape=jax.ShapeDtypeStruct((B, S, B_WIDTH), BF16),
        scratch_shapes=[pltpu.VMEM((B_HEADS // 2, S, LANES), F32)] + [pltpu.VMEM((S, LANES), F32)] * 6,
        compiler_params=pltpu.CompilerParams(
            dimension_semantics=("arbitrary", "arbitrary"), vmem_limit_bytes=VMEM_LIMIT),
        name="attention_b",
    )(*orders, bias, g)


def _out_mlp_kernel(x_ref, ma_ref, mb_ref, wo_ref, gm_ref, wup_ref, wdn_ref, gf_ref, o_ref):
    mix = jnp.concatenate([ma_ref[...], mb_ref[...]], axis=1)
    h = x_ref[...] + jnp.dot(mix, wo_ref[...], preferred_element_type=F32)
    hn = _rms(h, gm_ref[...]).astype(BF16)
    acc = h
    d_ff = wup_ref.shape[1]
    for c in range(d_ff // FF_CHUNK):
        sl = slice(c * FF_CHUNK, (c + 1) * FF_CHUNK)
        u = jnp.maximum(jnp.dot(hn, wup_ref[:, sl], preferred_element_type=F32), 0.0)
        acc = acc + jnp.dot((u * u).astype(BF16), wdn_ref[sl, :], preferred_element_type=F32)
    o_ref[...] = _rms(acc, gf_ref[...])


def _out_mlp(x2, ma, mb, wo, gm, wup, wdn, gf):
    N, D = x2.shape
    d_ff = wup.shape[1]
    tok = lambda t: (t, 0)
    const = lambda t: (0, 0)
    resident = functools.partial(pl.BlockSpec, index_map=const, pipeline_mode=pl.Buffered(1))
    return pl.pallas_call(
        _out_mlp_kernel,
        grid=(N // TM_OUT,),
        in_specs=[
            pl.BlockSpec((TM_OUT, D), tok),
            pl.BlockSpec((TM_OUT, A_WIDTH), tok),
            pl.BlockSpec((TM_OUT, B_WIDTH), tok),
            resident((A_WIDTH + B_WIDTH, D)),
            pl.BlockSpec((1, D), const),
            resident((D, d_ff)),
            resident((d_ff, D)),
            pl.BlockSpec((1, D), const),
        ],
        out_specs=pl.BlockSpec((TM_OUT, D), tok),
        out_shape=jax.ShapeDtypeStruct((N, D), F32),
        compiler_params=pltpu.CompilerParams(
            dimension_semantics=("arbitrary",), vmem_limit_bytes=VMEM_LIMIT),
        name="out_mlp",
    )(x2, ma, mb, wo, gm, wup, wdn, gf)


def kernel(x, attn_norm_g, w_in, q_norm_g, k_norm_g, rel_bias, out_norm_a_g, out_norm_b_g,
           w_out, mlp_norm_g, w_up, w_down, final_norm_g):
    B, S, D = x.shape
    assert w_in.shape[0] == 1, "the output kernel fuses the final norm: single layer only"
    bias = _bias_blocks(rel_bias)
    gq = jnp.tile(q_norm_g[0], LANES // HEAD_DIM).reshape(1, LANES)
    gk = jnp.tile(k_norm_g[0], LANES // HEAD_DIM).reshape(1, LANES)
    qat, ka, vat, *orders = _in_projection(
        x, attn_norm_g[0].reshape(1, D), w_in[0].astype(BF16), gq, gk)
    ma = _attention_a(qat, ka, vat, out_norm_a_g[0].reshape(1, A_WIDTH))
    mb = _attention_b(orders, bias, out_norm_b_g[0].reshape(1, B_WIDTH))
    out = _out_mlp(x.reshape(B * S, D), ma.reshape(B * S, A_WIDTH), mb.reshape(B * S, B_WIDTH),
                   w_out[0].astype(BF16), mlp_norm_g[0].reshape(1, D),
                   w_up[0].astype(BF16), w_down[0].astype(BF16), final_norm_g.reshape(1, D))
    return out.reshape(B, S, D)
```

```python
import functools
import math

import jax
import jax.numpy as jnp
import numpy as np
from jax import lax
from jax.experimental import pallas as pl
from jax.experimental.pallas import tpu as pltpu

HEAD_DIM = 64
A_Q_HEADS = 8
A_KV_HEADS = 2
B_HEADS = 8
GRID_W = 64
ROPE_THETA = 10000.0
DIL_BRANCHES = ((128, 1), (512, 4), (2048, 16))
N_BUCKETS = 32
BUCKET_MAX_DIST = 1024
EPS = 1e-6
NEG = -1e30

A_WIDTH = A_Q_HEADS * HEAD_DIM
A_KV_WIDTH = A_KV_HEADS * HEAD_DIM
B_WIDTH = B_HEADS * HEAD_DIM
LOG2E = math.log2(math.e)
Q_SCALE = HEAD_DIM ** -0.5 * LOG2E

LANES = 128
SUBLANES = 8
BF16_ROWS = 16
ROT = HEAD_DIM // 4
TM_IN = 512
TQ = 128
TQ_A = 256
KEY_CHUNK = 256
TM_OUT = 1024
FF_CHUNK = 1024
VMEM_LIMIT = 56 * 1024 * 1024

HALF_WINDOW = DIL_BRANCHES[0][0] // (2 * DIL_BRANCHES[0][1])
assert all(w // (2 * r) == HALF_WINDOW for w, r in DIL_BRANCHES)
KEY_WINDOW = TQ + 2 * HALF_WINDOW
BIAS_SPAN = 2 * KEY_WINDOW
N_VARIANTS = 3
B_UNROLL = 16

F32 = jnp.float32
BF16 = jnp.bfloat16


def _rms(x, g):
    return x * lax.rsqrt(jnp.mean(x * x, axis=-1, keepdims=True) + EPS) * g


def _inproj_kernel(x_ref, g_ref, w_ref, gq_ref, gk_ref, cos_ref, sin_ref, ones_ref,
                   qat_ref, ka_ref, vat_ref,
                   q1_ref, q4_ref, q16_ref, k1_ref, k4_ref, k16_ref, v1_ref, v4_ref, v16_ref,
                   sq_ref, sk_ref, sv_ref, mq_ref, mk_ref, mv_ref):
    n = _rms(x_ref[...], g_ref[...]).astype(BF16)
    cos = cos_ref[...]
    sin = sin_ref[...]
    ones2 = ones_ref[...]
    lane = lax.broadcasted_iota(jnp.int32, (TM_IN, LANES), 1)
    first_half = (lane % (2 * ROT)) < ROT

    def head_norm_rope(blk, gain):
        sq = blk * blk
        hi = sq.astype(BF16)
        lo = (sq - hi.astype(F32)).astype(BF16)
        ss = jnp.dot(jnp.concatenate([hi, lo], axis=1), ones2, preferred_element_type=F32)
        xn = blk * lax.rsqrt(ss * (1.0 / HEAD_DIM) + EPS) * gain
        partner = jnp.where(first_half, pltpu.roll(xn, LANES - ROT, 1), pltpu.roll(xn, ROT, 1))
        return xn * cos + partner * sin

    o_ka = A_WIDTH
    o_va = o_ka + A_KV_WIDTH
    o_qb = o_va + A_KV_WIDTH
    o_kb = o_qb + B_WIDTH
    o_vb = o_kb + B_WIDTH

    def stash(val, scr_ref, nat_ref):
        nat_ref[...] = val.astype(BF16)
        for c in range(B_WIDTH // LANES):
            scr_ref[c] = val[:, c * LANES:(c + 1) * LANES]

    def write_phases(scr_ref, mid_ref, ph_refs):
        (_, r1), (_, r2) = DIL_BRANCHES[1:]
        fine_ref, coarse_ref = ph_refs
        for c in range(B_WIDTH // LANES):
            sl = slice(c * LANES, (c + 1) * LANES)
            for p in range(r1):
                val = scr_ref[c, pl.ds(p, TM_IN // r1, stride=r1), :]
                fine_ref[p, :, sl] = val.astype(BF16)
                mid_ref[c, p] = val
            for p in range(r1):
                for q in range(r2 // r1):
                    coarse_ref[q * r1 + p, :, sl] = mid_ref[c, p, pl.ds(q, TM_IN // r2, stride=r1), :].astype(BF16)

    pq = jnp.dot(n, w_ref[:, 0:o_ka], preferred_element_type=F32)
    stash(jnp.dot(n, w_ref[:, o_qb:o_kb], preferred_element_type=F32) * Q_SCALE, sq_ref, q1_ref)
    gq = gq_ref[...]
    for j in range(A_WIDTH // LANES):
        sl = slice(j * LANES, (j + 1) * LANES)
        qat_ref[sl, :] = (head_norm_rope(pq[:, sl], gq) * Q_SCALE).T.astype(BF16)

    pk = jnp.dot(n, w_ref[:, o_ka:o_va], preferred_element_type=F32)
    pv = jnp.dot(n, w_ref[:, o_va:o_qb], preferred_element_type=F32)
    stash(jnp.dot(n, w_ref[:, o_kb:o_vb], preferred_element_type=F32), sk_ref, k1_ref)
    kr = head_norm_rope(pk, gk_ref[...])
    swapped = pltpu.roll(kr, HEAD_DIM, 1)
    low = lane < HEAD_DIM
    ka_ref[:, 0:LANES] = jnp.where(low, kr, swapped).astype(BF16)
    ka_ref[:, LANES:2 * LANES] = jnp.where(low, swapped, kr).astype(BF16)
    vat_ref[...] = pv.T.astype(BF16)
    write_phases(sq_ref, mq_ref, (q4_ref, q16_ref))

    stash(jnp.dot(n, w_ref[:, o_vb:], preferred_element_type=F32), sv_ref, v1_ref)
    write_phases(sk_ref, mk_ref, (k4_ref, k16_ref))
    write_phases(sv_ref, mv_ref, (v4_ref, v16_ref))


def _rope_tables(S):
    rows = S // GRID_W
    row = jnp.repeat(jnp.arange(rows, dtype=F32), GRID_W)
    col = jnp.tile(jnp.arange(GRID_W, dtype=F32), rows)
    n_freq = HEAD_DIM // 4
    inv = ROPE_THETA ** (-jnp.arange(n_freq, dtype=F32) / n_freq)
    ang_r = row[:, None] * inv[None, :]
    ang_c = col[:, None] * inv[None, :]
    cos_h = jnp.concatenate([jnp.cos(ang_r)] * 2 + [jnp.cos(ang_c)] * 2, axis=1)
    sin_h = jnp.concatenate([-jnp.sin(ang_r), jnp.sin(ang_r), -jnp.sin(ang_c), jnp.sin(ang_c)], axis=1)
    return jnp.tile(cos_h, (1, 2)), jnp.tile(sin_h, (1, 2))


def _in_projection(x, g, w_bf16, gq, gk):
    B, S, D = x.shape
    in_width = w_bf16.shape[1]
    cos, sin = _rope_tables(S)
    r = np.arange(2 * LANES)[:, None]
    c = np.arange(LANES)[None, :]
    ones2 = jnp.asarray(((r % LANES) // HEAD_DIM) == (c // HEAD_DIM), dtype=BF16)
    nt = S // TM_IN
    r_fine, r_coarse = DIL_BRANCHES[1][1], DIL_BRANCHES[2][1]
    assert r_coarse == r_fine * r_fine, "the coarse phase order is read as the fine order taken twice"
    tok = lambda t, b: (b, t, 0)
    phase = lambda t, b: (b, 0, t, 0)
    const = lambda t, b: (0, 0)
    orders_shape, orders_spec = [], []
    for _ in range(3):
        orders_shape.append(jax.ShapeDtypeStruct((B, S, B_WIDTH), BF16))
        orders_spec.append(pl.BlockSpec((None, TM_IN, B_WIDTH), tok))
        for _, rr in DIL_BRANCHES[1:]:
            orders_shape.append(jax.ShapeDtypeStruct((B, rr, S // rr, B_WIDTH), BF16))
            orders_spec.append(pl.BlockSpec((None, rr, TM_IN // rr, B_WIDTH), phase))
    out_shape = (
        jax.ShapeDtypeStruct((B, A_WIDTH, S), BF16),
        jax.ShapeDtypeStruct((B, S, 2 * A_KV_WIDTH), BF16),
        jax.ShapeDtypeStruct((B, A_KV_WIDTH, S), BF16),
        *orders_shape,
    )
    return pl.pallas_call(
        _inproj_kernel,
        grid=(nt, B),
        in_specs=[
            pl.BlockSpec((None, TM_IN, D), tok),
            pl.BlockSpec((1, D), const),
            pl.BlockSpec((D, in_width), const),
            pl.BlockSpec((1, LANES), const),
            pl.BlockSpec((1, LANES), const),
            pl.BlockSpec((TM_IN, LANES), lambda t, b: (t, 0)),
            pl.BlockSpec((TM_IN, LANES), lambda t, b: (t, 0)),
            pl.BlockSpec((2 * LANES, LANES), const),
        ],
        out_specs=(
            pl.BlockSpec((None, A_WIDTH, TM_IN), lambda t, b: (b, 0, t)),
            pl.BlockSpec((None, TM_IN, 2 * A_KV_WIDTH), tok),
            pl.BlockSpec((None, A_KV_WIDTH, TM_IN), lambda t, b: (b, 0, t)),
            *orders_spec,
        ),
        out_shape=out_shape,
        scratch_shapes=[pltpu.VMEM((B_WIDTH // LANES, TM_IN, LANES), F32)] * 3
                       + [pltpu.VMEM((B_WIDTH // LANES, r_fine, TM_IN // r_fine, LANES), F32)] * 3,
        compiler_params=pltpu.CompilerParams(
            dimension_semantics=("arbitrary", "arbitrary"), vmem_limit_bytes=VMEM_LIMIT),
        name="in_projection",
    )(x, g, w_bf16, gq, gk, cos, sin, ones2)


def _attn_a_kernel(qt_ref, k_ref, vt0_ref, vt1_ref, g_ref, o_ref, s0_ref, s1_ref, m1_ref, half_ref):
    group = A_Q_HEADS // A_KV_HEADS
    width = group * TQ_A
    S = k_ref.shape[0]
    top = lax.broadcasted_iota(jnp.int32, (LANES, TQ_A), 0) < HEAD_DIM
    ones = jnp.ones((BF16_ROWS, KEY_CHUNK), BF16)

    @pl.when(pl.program_id(0) == 0)
    def _():
        s1_ref[...] = jnp.zeros_like(s1_ref)
        m1_ref[...] = jnp.zeros_like(m1_ref)
        half_ref[...] = jnp.zeros_like(half_ref)

    def half_step(kv, s_new_ref, s_old_ref, old_max, vt_old_ref):
        cols = []
        for j in range(group // 2):
            r0 = (kv * group + 2 * j) * HEAD_DIM
            qt2 = qt_ref[r0:r0 + LANES, :]
            zero = jnp.zeros_like(qt2)
            cols += [jnp.where(top, qt2, zero), jnp.where(top, zero, qt2)]
        w = jnp.concatenate(cols, axis=1)
        run_max = jnp.full((SUBLANES, width), -jnp.inf, F32)
        acc = jnp.zeros((HEAD_DIM + BF16_ROWS, width), F32)
        for c in range(S // KEY_CHUNK):
            rows = slice(c * KEY_CHUNK, (c + 1) * KEY_CHUNK)
            st = jnp.dot(k_ref[rows, kv * LANES:(kv + 1) * LANES], w,
                         preferred_element_type=F32)
            p = jnp.exp2(s_old_ref[rows, :] - old_max).astype(BF16)
            vaug = jnp.concatenate([vt_old_ref[:, rows], ones], axis=0)
            acc = acc + jnp.dot(vaug, p, preferred_element_type=F32)
            s_new_ref[rows, :] = st
            run_max = jnp.maximum(run_max, st.reshape(KEY_CHUNK // SUBLANES, SUBLANES, width).max(axis=0))
        ot = acc[0:HEAD_DIM] * (1.0 / acc[HEAD_DIM:HEAD_DIM + 1])
        return jnp.max(run_max, axis=0, keepdims=True), ot

    def head_blocks(ot):
        return [jnp.concatenate([ot[:, 2 * j * TQ_A:(2 * j + 1) * TQ_A],
                                 ot[:, (2 * j + 1) * TQ_A:(2 * j + 2) * TQ_A]], axis=0).T
                for j in range(group // 2)]

    m0, late = half_step(0, s0_ref, s1_ref, m1_ref[...], vt1_ref)

    y = jnp.concatenate(head_blocks(half_ref[...]) + head_blocks(late), axis=1)
    o_ref[...] = _rms(y, g_ref[...]).astype(BF16)
    m1, early = half_step(1, s1_ref, s0_ref, m0, vt0_ref)
    m1_ref[...] = m1
    half_ref[...] = early


def _attention_a(qat, ka, vat, g):
    B, _, S = qat.shape
    assert A_KV_HEADS == 2, "the step skew pairs exactly two kv groups"
    width = (A_Q_HEADS // A_KV_HEADS) * TQ_A
    nq = S // TQ_A
    n_steps = B * nq
    cur = lambda t: jnp.minimum(t, n_steps - 1)
    prev = lambda t: jnp.maximum(t - 1, 0)
    return pl.pallas_call(
        _attn_a_kernel,
        grid=(n_steps + 1,),
        in_specs=[
            pl.BlockSpec((None, A_WIDTH, TQ_A), lambda t: (cur(t) // nq, 0, cur(t) % nq)),
            pl.BlockSpec((None, S, 2 * A_KV_WIDTH), lambda t: (cur(t) // nq, 0, 0)),
            pl.BlockSpec((None, HEAD_DIM, S), lambda t: (cur(t) // nq, 0, 0)),
            pl.BlockSpec((None, HEAD_DIM, S), lambda t: (prev(t) // nq, 1, 0)),
            pl.BlockSpec((1, A_WIDTH), lambda t: (0, 0)),
        ],
        out_specs=pl.BlockSpec((None, TQ_A, A_WIDTH), lambda t: (prev(t) // nq, prev(t) % nq, 0)),
        out_shape=jax.ShapeDtypeStruct((B, S, A_WIDTH), BF16),
        scratch_shapes=[pltpu.VMEM((S, width), F32), pltpu.VMEM((S, width), F32),
                        pltpu.VMEM((1, width), F32), pltpu.VMEM((HEAD_DIM, width), F32)],
        compiler_params=pltpu.CompilerParams(
            dimension_semantics=("arbitrary",), vmem_limit_bytes=VMEM_LIMIT),
        name="attention_a",
    )(qat, ka, vat, vat, g)


def _t5_bucket(rel):
    rel = np.asarray(rel, np.int32)
    nb_half = N_BUCKETS // 2
    max_exact = nb_half // 2
    ret = np.where(rel > 0, nb_half, 0)
    n = np.abs(rel)
    large = max_exact + (np.log(np.maximum(n, 1).astype(np.float32) / max_exact)
                         / math.log(BUCKET_MAX_DIST / max_exact)
                         * (nb_half - max_exact)).astype(np.int32)
    large = np.minimum(large, nb_half - 1)
    return (ret + np.where(n < max_exact, n, large)).astype(np.int32)


def _variant_offset(v):
    return (0, -HALF_WINDOW, -2 * HALF_WINDOW)[v]


def _bias_kernel(rbt_ref, code_ref, valid_ref, o_ref):
    h = pl.program_id(0)
    rb = rbt_ref[pl.ds(h, 1), :]
    for n in range(len(DIL_BRANCHES)):
        code = code_ref[n:n + 1, :]
        acc = jnp.zeros(code.shape, F32)
        for b in range(N_BUCKETS):
            acc = jnp.where(code == b, rb[:, b:b + 1], acc)
        f = jnp.where(valid_ref[n:n + 1, :] > 0, acc * LOG2E, NEG)
        rows = jnp.broadcast_to(f, (TQ, BIAS_SPAN))
        for v in range(N_VARIANTS):
            shift = (KEY_WINDOW - _variant_offset(v)) % BIAS_SPAN
            o_ref[n * N_VARIANTS + v] = pltpu.roll(rows, shift, 1, stride=1, stride_axis=0)[:, 0:KEY_WINDOW]


def _bias_blocks(rel_bias):
    rel = np.arange(BIAS_SPAN, dtype=np.int32) - KEY_WINDOW
    code = np.stack([_t5_bucket(rel * r) for _, r in DIL_BRANCHES])
    valid = np.broadcast_to((np.abs(rel) <= HALF_WINDOW).astype(np.int32), code.shape)
    n_blocks = len(DIL_BRANCHES) * N_VARIANTS
    const = lambda h: (0, 0)
    return pl.pallas_call(
        _bias_kernel,
        grid=(B_HEADS,),
        in_specs=[pl.BlockSpec((B_HEADS, N_BUCKETS), const),
                  pl.BlockSpec(code.shape, const),
                  pl.BlockSpec(code.shape, const)],
        out_specs=pl.BlockSpec((n_blocks, None, TQ, KEY_WINDOW), lambda h: (0, h, 0, 0)),
        out_shape=jax.ShapeDtypeStruct((n_blocks, B_HEADS, TQ, KEY_WINDOW), F32),
        compiler_params=pltpu.CompilerParams(dimension_semantics=("arbitrary",)),
        name="bias_blocks",
    )(rel_bias.T, jnp.asarray(code), jnp.asarray(valid))


def _split_pair(q2):
    lane = lax.broadcasted_iota(jnp.int32, q2.shape, 1)
    low = lane < HEAD_DIM
    zero = jnp.zeros_like(q2)
    return jnp.concatenate([jnp.where(low, q2, zero), jnp.where(low, zero, q2)], axis=0)


def _attn_b_kernel(q1_ref, q4_ref, q16_ref, k1_ref, k4_ref, k16_ref, v1_ref, v4_ref, v16_ref,
                   bias_ref, g_ref, o_ref, y_ref, *acc_refs):
    pair = pl.program_id(1)
    S = q1_ref.shape[0]
    n_blocks = S // TQ
    low = lax.broadcasted_iota(jnp.int32, (TQ, LANES), 1) < HEAD_DIM
    q_refs = (q1_ref, q4_ref, q16_ref)
    k_refs = (k1_ref, k4_ref, k16_ref)
    v_refs = (v1_ref, v4_ref, v16_ref)

    def merge(x2):
        return jnp.where(low, x2[0:TQ], x2[TQ:2 * TQ])

    def banded_block(q2, k2, v2, bias):
        s = lax.dot_general(_split_pair(q2), k2, (((1,), (1,)), ((), ())),
                            preferred_element_type=F32) + bias
        m = jnp.max(s, axis=1, keepdims=True)
        p = jnp.exp2(s - m).astype(BF16)
        vaug = jnp.concatenate([v2, jnp.ones_like(v2)], axis=1)
        oa = jnp.dot(p, vaug, preferred_element_type=F32)
        return merge(oa[:, 0:LANES]), merge(jnp.broadcast_to(m, (2 * TQ, LANES))), merge(oa[:, LANES:2 * LANES])

    def branch_block(n, i):
        r = DIL_BRANCHES[n][1]
        length = S // r
        per_phase = length // TQ
        if per_phase > 1:
            ph = i // per_phase
            blk = i - ph * per_phase
            variant = jnp.where(blk == 0, 0, jnp.where(blk == per_phase - 1, 2, 1))
            r0 = pl.multiple_of(blk * TQ, TQ)
            w0 = pl.multiple_of(jnp.clip(blk * TQ - HALF_WINDOW, 0, length - KEY_WINDOW), HALF_WINDOW)
            width = KEY_WINDOW
        else:
            ph, variant, r0, w0, width = i, 0, 0, 0, length
        tab = n * N_VARIANTS + variant
        bias = jnp.concatenate([bias_ref[tab, 2 * pair, :, 0:width],
                                bias_ref[tab, 2 * pair + 1, :, 0:width]], axis=0)
        if r == 1:
            q2 = q_refs[n][pl.ds(r0, TQ), :]
            k2 = k_refs[n][pl.ds(w0, width), :]
            v2 = v_refs[n][pl.ds(w0, width), :]
        else:
            q2 = q_refs[n][ph, pl.ds(r0, TQ), :]
            k2 = k_refs[n][ph, pl.ds(w0, width), :]
            v2 = v_refs[n][ph, pl.ds(w0, width), :]
        rows = pl.ds(r0 * r + ph, TQ, stride=r) if r > 1 else pl.ds(r0, TQ)
        return banded_block(q2, k2, v2, bias), rows

    def dilated_step(i, carry):
        for n in range(1, len(DIL_BRANCHES)):
            parts, rows = branch_block(n, i)
            for j, part in enumerate(parts):
                acc_refs[3 * (n - 1) + j][rows, :] = part
        return carry

    def dense_step(i, carry):
        parts, rows = branch_block(0, i)
        branches = [parts] + [tuple(acc_refs[3 * (n - 1) + j][rows, :] for j in range(3))
                              for n in range(1, len(DIL_BRANCHES))]
        top = functools.reduce(jnp.maximum, [m for _, m, _ in branches])
        scales = [jnp.exp2(m - top) for _, m, _ in branches]
        num = sum(w * o for w, (o, _, _) in zip(scales, branches))
        den = sum(w * d for w, (_, _, d) in zip(scales, branches))
        y_ref[pair, rows, :] = num * (1.0 / den)
        return carry

    lax.fori_loop(0, n_blocks, dilated_step, 0, unroll=B_UNROLL)
    lax.fori_loop(0, n_blocks, dense_step, 0, unroll=B_UNROLL)

    @pl.when(pair == B_HEADS // 2 - 1)
    def _():
        y = jnp.concatenate([y_ref[c] for c in range(B_HEADS // 2)], axis=1)
        o_ref[...] = _rms(y, g_ref[...]).astype(BF16)


def _attention_b(orders, bias, g):
    B, S, _ = orders[0].shape
    specs = []
    for a in orders:
        if a.ndim == 3:
            specs.append(pl.BlockSpec((None, S, LANES), lambda b, j: (b, 0, j)))
        else:
            specs.append(pl.BlockSpec((None, a.shape[1], a.shape[2], LANES), lambda b, j: (b, 0, 0, j)))
    return pl.pallas_call(
        _attn_b_kernel,
        grid=(B, B_HEADS // 2),
        in_specs=specs + [
            pl.BlockSpec(bias.shape, lambda b, j: (0, 0, 0, 0), pipeline_mode=pl.Buffered(1)),
            pl.BlockSpec((1, B_WIDTH), lambda b, j: (0, 0)),
        ],
        out_specs=pl.BlockSpec((None, S, B_WIDTH), lambda b, j: (b, 0, 0)),
        out_shape=jax.ShapeDtypeStruct((B, S, B_WIDTH), BF16),
        scratch_shapes=[pltpu.VMEM((B_HEADS // 2, S, LANES), F32)] + [pltpu.VMEM((S, LANES), F32)] * 6,
        compiler_params=pltpu.CompilerParams(
            dimension_semantics=("arbitrary", "arbitrary"), vmem_limit_bytes=VMEM_LIMIT),
        name="attention_b",
    )(*orders, bias, g)


def _out_mlp_kernel(x_ref, ma_ref, mb_ref, wo_ref, gm_ref, wup_ref, wdn_ref, gf_ref, o_ref):
    mix = jnp.concatenate([ma_ref[...], mb_ref[...]], axis=1)
    h = x_ref[...] + jnp.dot(mix, wo_ref[...], preferred_element_type=F32)
    hn = _rms(h, gm_ref[...]).astype(BF16)
    acc = h
    d_ff = wup_ref.shape[1]
    for c in range(d_ff // FF_CHUNK):
        sl = slice(c * FF_CHUNK, (c + 1) * FF_CHUNK)
        u = jnp.maximum(jnp.dot(hn, wup_ref[:, sl], preferred_element_type=F32), 0.0)
        acc = acc + jnp.dot((u * u).astype(BF16), wdn_ref[sl, :], preferred_element_type=F32)
    o_ref[...] = _rms(acc, gf_ref[...])


def _out_mlp(x2, ma, mb, wo, gm, wup, wdn, gf):
    N, D = x2.shape
    d_ff = wup.shape[1]
    tok = lambda t: (t, 0)
    const = lambda t: (0, 0)
    resident = functools.partial(pl.BlockSpec, index_map=const, pipeline_mode=pl.Buffered(1))
    return pl.pallas_call(
        _out_mlp_kernel,
        grid=(N // TM_OUT,),
        in_specs=[
            pl.BlockSpec((TM_OUT, D), tok),
            pl.BlockSpec((TM_OUT, A_WIDTH), tok),
            pl.BlockSpec((TM_OUT, B_WIDTH), tok),
            resident((A_WIDTH + B_WIDTH, D)),
            pl.BlockSpec((1, D), const),
            resident((D, d_ff)),
            resident((d_ff, D)),
            pl.BlockSpec((1, D), const),
        ],
        out_specs=pl.BlockSpec((TM_OUT, D), tok),
        out_shape=jax.ShapeDtypeStruct((N, D), F32),
        compiler_params=pltpu.CompilerParams(
            dimension_semantics=("arbitrary",), vmem_limit_bytes=VMEM_LIMIT),
        name="out_mlp",
    )(x2, ma, mb, wo, gm, wup, wdn, gf)


def kernel(x, attn_norm_g, w_in, q_norm_g, k_norm_g, rel_bias, out_norm_a_g, out_norm_b_g,
           w_out, mlp_norm_g, w_up, w_down, final_norm_g):
    B, S, D = x.shape
    assert w_in.shape[0] == 1, "the output kernel fuses the final norm: single layer only"
    bias = _bias_blocks(rel_bias)
    gq = jnp.tile(q_norm_g[0], LANES // HEAD_DIM).reshape(1, LANES)
    gk = jnp.tile(k_norm_g[0], LANES // HEAD_DIM).reshape(1, LANES)
    qat, ka, vat, *orders = _in_projection(
        x, attn_norm_g[0].reshape(1, D), w_in[0].astype(BF16), gq, gk)
    ma = _attention_a(qat, ka, vat, out_norm_a_g[0].reshape(1, A_WIDTH))
    mb = _attention_b(orders, bias, out_norm_b_g[0].reshape(1, B_WIDTH))
    out = _out_mlp(x.reshape(B * S, D), ma.reshape(B * S, A_WIDTH), mb.reshape(B * S, B_WIDTH),
                   w_out[0].astype(BF16), mlp_norm_g[0].reshape(1, D),
                   w_up[0].astype(BF16), w_down[0].astype(BF16), final_norm_g.reshape(1, D))
    return out.reshape(B, S, D)
```

```python
import functools
import math

import jax
import jax.numpy as jnp
import numpy as np
from jax import lax
from jax.experimental import pallas as pl
from jax.experimental.pallas import tpu as pltpu

HEAD_DIM = 64
A_Q_HEADS = 8
A_KV_HEADS = 2
B_HEADS = 8
GRID_W = 64
ROPE_THETA = 10000.0
DIL_BRANCHES = ((128, 1), (512, 4), (2048, 16))
N_BUCKETS = 32
BUCKET_MAX_DIST = 1024
EPS = 1e-6
NEG = -1e30

A_WIDTH = A_Q_HEADS * HEAD_DIM
A_KV_WIDTH = A_KV_HEADS * HEAD_DIM
B_WIDTH = B_HEADS * HEAD_DIM
LOG2E = math.log2(math.e)
Q_SCALE = HEAD_DIM ** -0.5 * LOG2E

LANES = 128
SUBLANES = 8
BF16_ROWS = 16
ROT = HEAD_DIM // 4
TM_IN = 512
TQ = 128
TQ_A = 256
KEY_CHUNK = 256
TM_OUT = 1024
FF_CHUNK = 1024
VMEM_LIMIT = 56 * 1024 * 1024

HALF_WINDOW = DIL_BRANCHES[0][0] // (2 * DIL_BRANCHES[0][1])
assert all(w // (2 * r) == HALF_WINDOW for w, r in DIL_BRANCHES)
KEY_WINDOW = TQ + 2 * HALF_WINDOW
BIAS_SPAN = 2 * KEY_WINDOW
N_VARIANTS = 3

F32 = jnp.float32
BF16 = jnp.bfloat16


def _rms(x, g):
    return x * lax.rsqrt(jnp.mean(x * x, axis=-1, keepdims=True) + EPS) * g


def _inproj_kernel(x_ref, g_ref, w_ref, gq_ref, gk_ref, cos_ref, sin_ref, ones_ref,
                   qat_ref, ka_ref, vat_ref,
                   q1_ref, q4_ref, q16_ref, k1_ref, k4_ref, k16_ref, v1_ref, v4_ref, v16_ref,
                   sq_ref, sk_ref, sv_ref, mq_ref, mk_ref, mv_ref):
    n = _rms(x_ref[...], g_ref[...]).astype(BF16)
    cos = cos_ref[...]
    sin = sin_ref[...]
    ones2 = ones_ref[...]
    lane = lax.broadcasted_iota(jnp.int32, (TM_IN, LANES), 1)
    first_half = (lane % (2 * ROT)) < ROT

    def head_norm_rope(blk, gain):
        sq = blk * blk
        hi = sq.astype(BF16)
        lo = (sq - hi.astype(F32)).astype(BF16)
        ss = jnp.dot(jnp.concatenate([hi, lo], axis=1), ones2, preferred_element_type=F32)
        xn = blk * lax.rsqrt(ss * (1.0 / HEAD_DIM) + EPS) * gain
        partner = jnp.where(first_half, pltpu.roll(xn, LANES - ROT, 1), pltpu.roll(xn, ROT, 1))
        return xn * cos + partner * sin

    o_ka = A_WIDTH
    o_va = o_ka + A_KV_WIDTH
    o_qb = o_va + A_KV_WIDTH
    o_kb = o_qb + B_WIDTH
    o_vb = o_kb + B_WIDTH

    def stash(val, scr_ref, nat_ref):
        nat_ref[...] = val.astype(BF16)
        for c in range(B_WIDTH // LANES):
            scr_ref[c] = val[:, c * LANES:(c + 1) * LANES]

    def write_phases(scr_ref, mid_ref, ph_refs):
        (_, r1), (_, r2) = DIL_BRANCHES[1:]
        fine_ref, coarse_ref = ph_refs
        for c in range(B_WIDTH // LANES):
            sl = slice(c * LANES, (c + 1) * LANES)
            for p in range(r1):
                val = scr_ref[c, pl.ds(p, TM_IN // r1, stride=r1), :]
                fine_ref[p, :, sl] = val.astype(BF16)
                mid_ref[c, p] = val
            for p in range(r1):
                for q in range(r2 // r1):
                    coarse_ref[q * r1 + p, :, sl] = mid_ref[c, p, pl.ds(q, TM_IN // r2, stride=r1), :].astype(BF16)

    pq = jnp.dot(n, w_ref[:, 0:o_ka], preferred_element_type=F32)
    stash(jnp.dot(n, w_ref[:, o_qb:o_kb], preferred_element_type=F32) * Q_SCALE, sq_ref, q1_ref)
    gq = gq_ref[...]
    for j in range(A_WIDTH // LANES):
        sl = slice(j * LANES, (j + 1) * LANES)
        qat_ref[sl, :] = (head_norm_rope(pq[:, sl], gq) * Q_SCALE).T.astype(BF16)

    pk = jnp.dot(n, w_ref[:, o_ka:o_va], preferred_element_type=F32)
    pv = jnp.dot(n, w_ref[:, o_va:o_qb], preferred_element_type=F32)
    stash(jnp.dot(n, w_ref[:, o_kb:o_vb], preferred_element_type=F32), sk_ref, k1_ref)
    kr = head_norm_rope(pk, gk_ref[...])
    swapped = pltpu.roll(kr, HEAD_DIM, 1)
    low = lane < HEAD_DIM
    ka_ref[:, 0:LANES] = jnp.where(low, kr, swapped).astype(BF16)
    ka_ref[:, LANES:2 * LANES] = jnp.where(low, swapped, kr).astype(BF16)
    vat_ref[...] = pv.T.astype(BF16)
    write_phases(sq_ref, mq_ref, (q4_ref, q16_ref))

    stash(jnp.dot(n, w_ref[:, o_vb:], preferred_element_type=F32), sv_ref, v1_ref)
    write_phases(sk_ref, mk_ref, (k4_ref, k16_ref))
    write_phases(sv_ref, mv_ref, (v4_ref, v16_ref))


def _rope_tables(S):
    rows = S // GRID_W
    row = jnp.repeat(jnp.arange(rows, dtype=F32), GRID_W)
    col = jnp.tile(jnp.arange(GRID_W, dtype=F32), rows)
    n_freq = HEAD_DIM // 4
    inv = ROPE_THETA ** (-jnp.arange(n_freq, dtype=F32) / n_freq)
    ang_r = row[:, None] * inv[None, :]
    ang_c = col[:, None] * inv[None, :]
    cos_h = jnp.concatenate([jnp.cos(ang_r)] * 2 + [jnp.cos(ang_c)] * 2, axis=1)
    sin_h = jnp.concatenate([-jnp.sin(ang_r), jnp.sin(ang_r), -jnp.sin(ang_c), jnp.sin(ang_c)], axis=1)
    return jnp.tile(cos_h, (1, 2)), jnp.tile(sin_h, (1, 2))


def _in_projection(x, g, w_bf16, gq, gk):
    B, S, D = x.shape
    in_width = w_bf16.shape[1]
    cos, sin = _rope_tables(S)
    r = np.arange(2 * LANES)[:, None]
    c = np.arange(LANES)[None, :]
    ones2 = jnp.asarray(((r % LANES) // HEAD_DIM) == (c // HEAD_DIM), dtype=BF16)
    nt = S // TM_IN
    r_fine, r_coarse = DIL_BRANCHES[1][1], DIL_BRANCHES[2][1]
    assert r_coarse == r_fine * r_fine, "the coarse phase order is read as the fine order taken twice"
    tok = lambda t, b: (b, t, 0)
    phase = lambda t, b: (b, 0, t, 0)
    const = lambda t, b: (0, 0)
    orders_shape, orders_spec = [], []
    for _ in range(3):
        orders_shape.append(jax.ShapeDtypeStruct((B, S, B_WIDTH), BF16))
        orders_spec.append(pl.BlockSpec((None, TM_IN, B_WIDTH), tok))
        for _, rr in DIL_BRANCHES[1:]:
            orders_shape.append(jax.ShapeDtypeStruct((B, rr, S // rr, B_WIDTH), BF16))
            orders_spec.append(pl.BlockSpec((None, rr, TM_IN // rr, B_WIDTH), phase))
    out_shape = (
        jax.ShapeDtypeStruct((B, A_WIDTH, S), BF16),
        jax.ShapeDtypeStruct((B, S, 2 * A_KV_WIDTH), BF16),
        jax.ShapeDtypeStruct((B, A_KV_WIDTH, S), BF16),
        *orders_shape,
    )
    return pl.pallas_call(
        _inproj_kernel,
        grid=(nt, B),
        in_specs=[
            pl.BlockSpec((None, TM_IN, D), tok),
            pl.BlockSpec((1, D), const),
            pl.BlockSpec((D, in_width), const),
            pl.BlockSpec((1, LANES), const),
            pl.BlockSpec((1, LANES), const),
            pl.BlockSpec((TM_IN, LANES), lambda t, b: (t, 0)),
            pl.BlockSpec((TM_IN, LANES), lambda t, b: (t, 0)),
            pl.BlockSpec((2 * LANES, LANES), const),
        ],
        out_specs=(
            pl.BlockSpec((None, A_WIDTH, TM_IN), lambda t, b: (b, 0, t)),
            pl.BlockSpec((None, TM_IN, 2 * A_KV_WIDTH), tok),
            pl.BlockSpec((None, A_KV_WIDTH, TM_IN), lambda t, b: (b, 0, t)),
            *orders_spec,
        ),
        out_shape=out_shape,
        scratch_shapes=[pltpu.VMEM((B_WIDTH // LANES, TM_IN, LANES), F32)] * 3
                       + [pltpu.VMEM((B_WIDTH // LANES, r_fine, TM_IN // r_fine, LANES), F32)] * 3,
        compiler_params=pltpu.CompilerParams(
            dimension_semantics=("arbitrary", "arbitrary"), vmem_limit_bytes=VMEM_LIMIT),
        name="in_projection",
    )(x, g, w_bf16, gq, gk, cos, sin, ones2)


def _attn_a_kernel(qt_ref, k_ref, vt0_ref, vt1_ref, g_ref, o_ref, s0_ref, s1_ref, m1_ref, half_ref):
    group = A_Q_HEADS // A_KV_HEADS
    width = group * TQ_A
    S = k_ref.shape[0]
    top = lax.broadcasted_iota(jnp.int32, (LANES, TQ_A), 0) < HEAD_DIM
    ones = jnp.ones((BF16_ROWS, KEY_CHUNK), BF16)

    @pl.when(pl.program_id(0) == 0)
    def _():
        s1_ref[...] = jnp.zeros_like(s1_ref)
        m1_ref[...] = jnp.zeros_like(m1_ref)
        half_ref[...] = jnp.zeros_like(half_ref)

    def half_step(kv, s_new_ref, s_old_ref, old_max, vt_old_ref):
        cols = []
        for j in range(group // 2):
            r0 = (kv * group + 2 * j) * HEAD_DIM
            qt2 = qt_ref[r0:r0 + LANES, :]
            zero = jnp.zeros_like(qt2)
            cols += [jnp.where(top, qt2, zero), jnp.where(top, zero, qt2)]
        w = jnp.concatenate(cols, axis=1)
        run_max = jnp.full((SUBLANES, width), -jnp.inf, F32)
        acc = jnp.zeros((HEAD_DIM + BF16_ROWS, width), F32)
        for c in range(S // KEY_CHUNK):
            rows = slice(c * KEY_CHUNK, (c + 1) * KEY_CHUNK)
            st = jnp.dot(k_ref[rows, kv * LANES:(kv + 1) * LANES], w,
                         preferred_element_type=F32)
            p = jnp.exp2(s_old_ref[rows, :] - old_max).astype(BF16)
            vaug = jnp.concatenate([vt_old_ref[:, rows], ones], axis=0)
            acc = acc + jnp.dot(vaug, p, preferred_element_type=F32)
            s_new_ref[rows, :] = st
            run_max = jnp.maximum(run_max, st.reshape(KEY_CHUNK // SUBLANES, SUBLANES, width).max(axis=0))
        ot = acc[0:HEAD_DIM] * (1.0 / acc[HEAD_DIM:HEAD_DIM + 1])
        return jnp.max(run_max, axis=0, keepdims=True), ot

    def head_blocks(ot):
        return [jnp.concatenate([ot[:, 2 * j * TQ_A:(2 * j + 1) * TQ_A],
                                 ot[:, (2 * j + 1) * TQ_A:(2 * j + 2) * TQ_A]], axis=0).T
                for j in range(group // 2)]

    m0, late = half_step(0, s0_ref, s1_ref, m1_ref[...], vt1_ref)

    y = jnp.concatenate(head_blocks(half_ref[...]) + head_blocks(late), axis=1)
    o_ref[...] = _rms(y, g_ref[...]).astype(BF16)
    m1, early = half_step(1, s1_ref, s0_ref, m0, vt0_ref)
    m1_ref[...] = m1
    half_ref[...] = early


def _attention_a(qat, ka, vat, g):
    B, _, S = qat.shape
    assert A_KV_HEADS == 2, "the step skew pairs exactly two kv groups"
    width = (A_Q_HEADS // A_KV_HEADS) * TQ_A
    nq = S // TQ_A
    n_steps = B * nq
    cur = lambda t: jnp.minimum(t, n_steps - 1)
    prev = lambda t: jnp.maximum(t - 1, 0)
    return pl.pallas_call(
        _attn_a_kernel,
        grid=(n_steps + 1,),
        in_specs=[
            pl.BlockSpec((None, A_WIDTH, TQ_A), lambda t: (cur(t) // nq, 0, cur(t) % nq)),
            pl.BlockSpec((None, S, 2 * A_KV_WIDTH), lambda t: (cur(t) // nq, 0, 0)),
            pl.BlockSpec((None, HEAD_DIM, S), lambda t: (cur(t) // nq, 0, 0)),
            pl.BlockSpec((None, HEAD_DIM, S), lambda t: (prev(t) // nq, 1, 0)),
            pl.BlockSpec((1, A_WIDTH), lambda t: (0, 0)),
        ],
        out_specs=pl.BlockSpec((None, TQ_A, A_WIDTH), lambda t: (prev(t) // nq, prev(t) % nq, 0)),
        out_shape=jax.ShapeDtypeStruct((B, S, A_WIDTH), BF16),
        scratch_shapes=[pltpu.VMEM((S, width), F32), pltpu.VMEM((S, width), F32),
                        pltpu.VMEM((1, width), F32), pltpu.VMEM((HEAD_DIM, width), F32)],
        compiler_params=pltpu.CompilerParams(
            dimension_semantics=("arbitrary",), vmem_limit_bytes=VMEM_LIMIT),
        name="attention_a",
    )(qat, ka, vat, vat, g)


def _t5_bucket(rel):
    rel = np.asarray(rel, np.int32)
    nb_half = N_BUCKETS // 2
    max_exact = nb_half // 2
    ret = np.where(rel > 0, nb_half, 0)
    n = np.abs(rel)
    large = max_exact + (np.log(np.maximum(n, 1).astype(np.float32) / max_exact)
                         / math.log(BUCKET_MAX_DIST / max_exact)
                         * (nb_half - max_exact)).astype(np.int32)
    large = np.minimum(large, nb_half - 1)
    return (ret + np.where(n < max_exact, n, large)).astype(np.int32)


def _variant_offset(v):
    return (0, -HALF_WINDOW, -2 * HALF_WINDOW)[v]


def _bias_kernel(rbt_ref, code_ref, valid_ref, o_ref):
    h = pl.program_id(0)
    rb = rbt_ref[pl.ds(h, 1), :]
    for n in range(len(DIL_BRANCHES)):
        code = code_ref[n:n + 1, :]
        acc = jnp.zeros(code.shape, F32)
        for b in range(N_BUCKETS):
            acc = jnp.where(code == b, rb[:, b:b + 1], acc)
        f = jnp.where(valid_ref[n:n + 1, :] > 0, acc * LOG2E, NEG)
        rows = jnp.broadcast_to(f, (TQ, BIAS_SPAN))
        for v in range(N_VARIANTS):
            shift = (KEY_WINDOW - _variant_offset(v)) % BIAS_SPAN
            o_ref[n * N_VARIANTS + v] = pltpu.roll(rows, shift, 1, stride=1, stride_axis=0)[:, 0:KEY_WINDOW]


def _bias_blocks(rel_bias):
    rel = np.arange(BIAS_SPAN, dtype=np.int32) - KEY_WINDOW
    code = np.stack([_t5_bucket(rel * r) for _, r in DIL_BRANCHES])
    valid = np.broadcast_to((np.abs(rel) <= HALF_WINDOW).astype(np.int32), code.shape)
    n_blocks = len(DIL_BRANCHES) * N_VARIANTS
    const = lambda h: (0, 0)
    return pl.pallas_call(
        _bias_kernel,
        grid=(B_HEADS,),
        in_specs=[pl.BlockSpec((B_HEADS, N_BUCKETS), const),
                  pl.BlockSpec(code.shape, const),
                  pl.BlockSpec(code.shape, const)],
        out_specs=pl.BlockSpec((n_blocks, None, TQ, KEY_WINDOW), lambda h: (0, h, 0, 0)),
        out_shape=jax.ShapeDtypeStruct((n_blocks, B_HEADS, TQ, KEY_WINDOW), F32),
        compiler_params=pltpu.CompilerParams(dimension_semantics=("arbitrary",)),
        name="bias_blocks",
    )(rel_bias.T, jnp.asarray(code), jnp.asarray(valid))


def _split_pair(q2):
    lane = lax.broadcasted_iota(jnp.int32, q2.shape, 1)
    low = lane < HEAD_DIM
    zero = jnp.zeros_like(q2)
    return jnp.concatenate([jnp.where(low, q2, zero), jnp.where(low, zero, q2)], axis=0)


def _attn_b_kernel(q1_ref, q4_ref, q16_ref, k1_ref, k4_ref, k16_ref, v1_ref, v4_ref, v16_ref,
                   bias_ref, g_ref, o_ref, y_ref, *acc_refs):
    pair = pl.program_id(1)
    S = q1_ref.shape[0]
    n_blocks = S // TQ
    low = lax.broadcasted_iota(jnp.int32, (TQ, LANES), 1) < HEAD_DIM
    q_refs = (q1_ref, q4_ref, q16_ref)
    k_refs = (k1_ref, k4_ref, k16_ref)
    v_refs = (v1_ref, v4_ref, v16_ref)

    def merge(x2):
        return jnp.where(low, x2[0:TQ], x2[TQ:2 * TQ])

    def banded_block(q2, k2, v2, bias):
        s = lax.dot_general(_split_pair(q2), k2, (((1,), (1,)), ((), ())),
                            preferred_element_type=F32) + bias
        m = jnp.max(s, axis=1, keepdims=True)
        p = jnp.exp2(s - m).astype(BF16)
        vaug = jnp.concatenate([v2, jnp.ones_like(v2)], axis=1)
        oa = jnp.dot(p, vaug, preferred_element_type=F32)
        return merge(oa[:, 0:LANES]), merge(jnp.broadcast_to(m, (2 * TQ, LANES))), merge(oa[:, LANES:2 * LANES])

    def branch_block(n, i):
        r = DIL_BRANCHES[n][1]
        length = S // r
        per_phase = length // TQ
        ph, blk = divmod(i, per_phase)
        if per_phase > 1:
            variant = 0 if blk == 0 else (2 if blk == per_phase - 1 else 1)
            w0 = min(max(blk * TQ - HALF_WINDOW, 0), length - KEY_WINDOW)
            width = KEY_WINDOW
        else:
            variant, w0, width = 0, 0, length
        r0 = blk * TQ
        tab = n * N_VARIANTS + variant
        bias = jnp.concatenate([bias_ref[tab, 2 * pair, :, 0:width],
                                bias_ref[tab, 2 * pair + 1, :, 0:width]], axis=0)
        lead = () if r == 1 else (ph,)
        q2 = q_refs[n][lead + (slice(r0, r0 + TQ), slice(None))]
        k2 = k_refs[n][lead + (slice(w0, w0 + width), slice(None))]
        v2 = v_refs[n][lead + (slice(w0, w0 + width), slice(None))]
        return banded_block(q2, k2, v2, bias), ph, r0

    for i in range(n_blocks):
        for n in range(1, len(DIL_BRANCHES)):
            r = DIL_BRANCHES[n][1]
            parts, ph, r0 = branch_block(n, i)
            rows = pl.ds(r0 * _row_pitch(r) + ph, TQ, stride=_row_pitch(r))
            for j, part in enumerate(parts):
                acc_refs[3 * (n - 1) + j][rows, :] = part

    def natural_rows(ref, r, r0):
        pitch = _row_pitch(r)
        if pitch == r:
            return ref[r0:r0 + TQ, :]
        runs = [ref[(r0 // r + q) * pitch:(r0 // r + q) * pitch + r, :] for q in range(TQ // r)]
        return jnp.concatenate(runs, axis=0)

    for i in range(n_blocks):
        parts, _, r0 = branch_block(0, i)
        branches = [parts] + [tuple(natural_rows(acc_refs[3 * (n - 1) + j], DIL_BRANCHES[n][1], r0) for j in range(3))
                              for n in range(1, len(DIL_BRANCHES))]
        top = functools.reduce(jnp.maximum, [m for _, m, _ in branches])
        scales = [jnp.exp2(m - top) for _, m, _ in branches]
        num = sum(w * o for w, (o, _, _) in zip(scales, branches))
        den = sum(w * d for w, (_, _, d) in zip(scales, branches))
        y_ref[pair, r0:r0 + TQ, :] = num * (1.0 / den)

    @pl.when(pair == B_HEADS // 2 - 1)
    def _():
        y = jnp.concatenate([y_ref[c] for c in range(B_HEADS // 2)], axis=1)
        o_ref[...] = _rms(y, g_ref[...]).astype(BF16)


def _row_pitch(r):
    return r + 1 if r % SUBLANES == 0 else r


def _attention_b(orders, bias, g):
    B, S, _ = orders[0].shape
    specs = []
    for a in orders:
        if a.ndim == 3:
            specs.append(pl.BlockSpec((None, S, LANES), lambda b, j: (b, 0, j)))
        else:
            specs.append(pl.BlockSpec((None, a.shape[1], a.shape[2], LANES), lambda b, j: (b, 0, 0, j)))
    return pl.pallas_call(
        _attn_b_kernel,
        grid=(B, B_HEADS // 2),
        in_specs=specs + [
            pl.BlockSpec(bias.shape, lambda b, j: (0, 0, 0, 0), pipeline_mode=pl.Buffered(1)),
            pl.BlockSpec((1, B_WIDTH), lambda b, j: (0, 0)),
        ],
        out_specs=pl.BlockSpec((None, S, B_WIDTH), lambda b, j: (b, 0, 0)),
        out_shape=jax.ShapeDtypeStruct((B, S, B_WIDTH), BF16),
        scratch_shapes=[pltpu.VMEM((B_HEADS // 2, S, LANES), F32)]
                       + [pltpu.VMEM((S // r * _row_pitch(r), LANES), F32) for _, r in DIL_BRANCHES[1:] for _ in range(3)],
        compiler_params=pltpu.CompilerParams(
            dimension_semantics=("arbitrary", "arbitrary"), vmem_limit_bytes=VMEM_LIMIT),
        name="attention_b",
    )(*orders, bias, g)


def _out_mlp_kernel(x_ref, ma_ref, mb_ref, wo_ref, gm_ref, wup_ref, wdn_ref, gf_ref, o_ref):
    mix = jnp.concatenate([ma_ref[...], mb_ref[...]], axis=1)
    h = x_ref[...] + jnp.dot(mix, wo_ref[...], preferred_element_type=F32)
    hn = _rms(h, gm_ref[...]).astype(BF16)
    acc = h
    d_ff = wup_ref.shape[1]
    for c in range(d_ff // FF_CHUNK):
        sl = slice(c * FF_CHUNK, (c + 1) * FF_CHUNK)
        u = jnp.maximum(jnp.dot(hn, wup_ref[:, sl], preferred_element_type=F32), 0.0)
        acc = acc + jnp.dot((u * u).astype(BF16), wdn_ref[sl, :], preferred_element_type=F32)
    o_ref[...] = _rms(acc, gf_ref[...])


def _out_mlp(x2, ma, mb, wo, gm, wup, wdn, gf):
    N, D = x2.shape
    d_ff = wup.shape[1]
    tok = lambda t: (t, 0)
    const = lambda t: (0, 0)
    resident = functools.partial(pl.BlockSpec, index_map=const, pipeline_mode=pl.Buffered(1))
    return pl.pallas_call(
        _out_mlp_kernel,
        grid=(N // TM_OUT,),
        in_specs=[
            pl.BlockSpec((TM_OUT, D), tok),
            pl.BlockSpec((TM_OUT, A_WIDTH), tok),
            pl.BlockSpec((TM_OUT, B_WIDTH), tok),
            resident((A_WIDTH + B_WIDTH, D)),
            pl.BlockSpec((1, D), const),
            resident((D, d_ff)),
            resident((d_ff, D)),
            pl.BlockSpec((1, D), const),
        ],
        out_specs=pl.BlockSpec((TM_OUT, D), tok),
        out_shape=jax.ShapeDtypeStruct((N, D), F32),
        compiler_params=pltpu.CompilerParams(
            dimension_semantics=("arbitrary",), vmem_limit_bytes=VMEM_LIMIT),
        name="out_mlp",
    )(x2, ma, mb, wo, gm, wup, wdn, gf)


def kernel(x, attn_norm_g, w_in, q_norm_g, k_norm_g, rel_bias, out_norm_a_g, out_norm_b_g,
           w_out, mlp_norm_g, w_up, w_down, final_norm_g):
    B, S, D = x.shape
    assert w_in.shape[0] == 1, "the output kernel fuses the final norm: single layer only"
    bias = _bias_blocks(rel_bias)
    gq = jnp.tile(q_norm_g[0], LANES // HEAD_DIM).reshape(1, LANES)
    gk = jnp.tile(k_norm_g[0], LANES // HEAD_DIM).reshape(1, LANES)
    qat, ka, vat, *orders = _in_projection(
        x, attn_norm_g[0].reshape(1, D), w_in[0].astype(BF16), gq, gk)
    ma = _attention_a(qat, ka, vat, out_norm_a_g[0].reshape(1, A_WIDTH))
    mb = _attention_b(orders, bias, out_norm_b_g[0].reshape(1, B_WIDTH))
    out = _out_mlp(x.reshape(B * S, D), ma.reshape(B * S, A_WIDTH), mb.reshape(B * S, B_WIDTH),
                   w_out[0].astype(BF16), mlp_norm_g[0].reshape(1, D),
                   w_up[0].astype(BF16), w_down[0].astype(BF16), final_norm_g.reshape(1, D))
    return out.reshape(B, S, D)
```

```python
import functools
import math

import jax
import jax.numpy as jnp
import numpy as np
from jax import lax
from jax.experimental import pallas as pl
from jax.experimental.pallas import tpu as pltpu

HEAD_DIM = 64
A_Q_HEADS = 8
A_KV_HEADS = 2
B_HEADS = 8
GRID_W = 64
ROPE_THETA = 10000.0
DIL_BRANCHES = ((128, 1), (512, 4), (2048, 16))
N_BUCKETS = 32
BUCKET_MAX_DIST = 1024
EPS = 1e-6
NEG = -1e30

A_WIDTH = A_Q_HEADS * HEAD_DIM
A_KV_WIDTH = A_KV_HEADS * HEAD_DIM
B_WIDTH = B_HEADS * HEAD_DIM
LOG2E = math.log2(math.e)
Q_SCALE = HEAD_DIM ** -0.5 * LOG2E

LANES = 128
SUBLANES = 8
BF16_ROWS = 16
ROT = HEAD_DIM // 4
TM_IN = 1024
TQ = 128
TQ_A = 256
KEY_CHUNK = 256
TM_OUT = 1024
FF_CHUNK = 1024
VMEM_LIMIT = 56 * 1024 * 1024

HALF_WINDOW = DIL_BRANCHES[0][0] // (2 * DIL_BRANCHES[0][1])
assert all(w // (2 * r) == HALF_WINDOW for w, r in DIL_BRANCHES)
KEY_WINDOW = TQ + 2 * HALF_WINDOW
BIAS_SPAN = 2 * KEY_WINDOW
N_VARIANTS = 3

F32 = jnp.float32
BF16 = jnp.bfloat16


def _rms(x, g):
    return x * lax.rsqrt(jnp.mean(x * x, axis=-1, keepdims=True) + EPS) * g


def _inproj_kernel(x_ref, g_ref, w_ref, gq_ref, gk_ref, cos_ref, sin_ref, ones_ref,
                   qat_ref, ka_ref, vat_ref,
                   q1_ref, q4_ref, q16_ref, k1_ref, k4_ref, k16_ref, v1_ref, v4_ref, v16_ref,
                   sq_ref, sk_ref, sv_ref, mq_ref, mk_ref, mv_ref):
    n = _rms(x_ref[...], g_ref[...]).astype(BF16)
    cos = cos_ref[...]
    sin = sin_ref[...]
    ones2 = ones_ref[...]
    lane = lax.broadcasted_iota(jnp.int32, (TM_IN, LANES), 1)
    first_half = (lane % (2 * ROT)) < ROT

    def head_norm_rope(blk, gain):
        sq = blk * blk
        hi = sq.astype(BF16)
        lo = (sq - hi.astype(F32)).astype(BF16)
        ss = jnp.dot(jnp.concatenate([hi, lo], axis=1), ones2, preferred_element_type=F32)
        xn = blk * lax.rsqrt(ss * (1.0 / HEAD_DIM) + EPS) * gain
        partner = jnp.where(first_half, pltpu.roll(xn, LANES - ROT, 1), pltpu.roll(xn, ROT, 1))
        return xn * cos + partner * sin

    o_ka = A_WIDTH
    o_va = o_ka + A_KV_WIDTH
    o_qb = o_va + A_KV_WIDTH
    o_kb = o_qb + B_WIDTH
    o_vb = o_kb + B_WIDTH

    def stash(val, scr_ref, nat_ref):
        nat_ref[...] = val.astype(BF16)
        for c in range(B_WIDTH // LANES):
            scr_ref[c] = val[:, c * LANES:(c + 1) * LANES]

    def write_phases(scr_ref, mid_ref, ph_refs):
        (_, r1), (_, r2) = DIL_BRANCHES[1:]
        fine_ref, coarse_ref = ph_refs
        for c in range(B_WIDTH // LANES):
            sl = slice(c * LANES, (c + 1) * LANES)
            for p in range(r1):
                val = scr_ref[c, pl.ds(p, TM_IN // r1, stride=r1), :]
                fine_ref[p, :, sl] = val.astype(BF16)
                mid_ref[c, p] = val
            for p in range(r1):
                for q in range(r2 // r1):
                    coarse_ref[q * r1 + p, :, sl] = mid_ref[c, p, pl.ds(q, TM_IN // r2, stride=r1), :].astype(BF16)

    pq = jnp.dot(n, w_ref[:, 0:o_ka], preferred_element_type=F32)
    stash(jnp.dot(n, w_ref[:, o_qb:o_kb], preferred_element_type=F32) * Q_SCALE, sq_ref, q1_ref)
    gq = gq_ref[...]
    for j in range(A_WIDTH // LANES):
        sl = slice(j * LANES, (j + 1) * LANES)
        qat_ref[sl, :] = (head_norm_rope(pq[:, sl], gq) * Q_SCALE).T.astype(BF16)

    pk = jnp.dot(n, w_ref[:, o_ka:o_va], preferred_element_type=F32)
    pv = jnp.dot(n, w_ref[:, o_va:o_qb], preferred_element_type=F32)
    stash(jnp.dot(n, w_ref[:, o_kb:o_vb], preferred_element_type=F32), sk_ref, k1_ref)
    kr = head_norm_rope(pk, gk_ref[...])
    swapped = pltpu.roll(kr, HEAD_DIM, 1)
    low = lane < HEAD_DIM
    ka_ref[:, 0:LANES] = jnp.where(low, kr, swapped).astype(BF16)
    ka_ref[:, LANES:2 * LANES] = jnp.where(low, swapped, kr).astype(BF16)
    vat_ref[...] = pv.T.astype(BF16)
    write_phases(sq_ref, mq_ref, (q4_ref, q16_ref))

    stash(jnp.dot(n, w_ref[:, o_vb:], preferred_element_type=F32), sv_ref, v1_ref)
    write_phases(sk_ref, mk_ref, (k4_ref, k16_ref))
    write_phases(sv_ref, mv_ref, (v4_ref, v16_ref))


def _rope_tables(S):
    rows = S // GRID_W
    row = jnp.repeat(jnp.arange(rows, dtype=F32), GRID_W)
    col = jnp.tile(jnp.arange(GRID_W, dtype=F32), rows)
    n_freq = HEAD_DIM // 4
    inv = ROPE_THETA ** (-jnp.arange(n_freq, dtype=F32) / n_freq)
    ang_r = row[:, None] * inv[None, :]
    ang_c = col[:, None] * inv[None, :]
    cos_h = jnp.concatenate([jnp.cos(ang_r)] * 2 + [jnp.cos(ang_c)] * 2, axis=1)
    sin_h = jnp.concatenate([-jnp.sin(ang_r), jnp.sin(ang_r), -jnp.sin(ang_c), jnp.sin(ang_c)], axis=1)
    return jnp.tile(cos_h, (1, 2)), jnp.tile(sin_h, (1, 2))


def _in_projection(x, g, w_bf16, gq, gk):
    B, S, D = x.shape
    in_width = w_bf16.shape[1]
    cos, sin = _rope_tables(S)
    r = np.arange(2 * LANES)[:, None]
    c = np.arange(LANES)[None, :]
    ones2 = jnp.asarray(((r % LANES) // HEAD_DIM) == (c // HEAD_DIM), dtype=BF16)
    nt = S // TM_IN
    r_fine, r_coarse = DIL_BRANCHES[1][1], DIL_BRANCHES[2][1]
    assert r_coarse == r_fine * r_fine, "the coarse phase order is read as the fine order taken twice"
    tok = lambda t, b: (b, t, 0)
    phase = lambda t, b: (b, 0, t, 0)
    const = lambda t, b: (0, 0)
    orders_shape, orders_spec = [], []
    for _ in range(3):
        orders_shape.append(jax.ShapeDtypeStruct((B, S, B_WIDTH), BF16))
        orders_spec.append(pl.BlockSpec((None, TM_IN, B_WIDTH), tok))
        for _, rr in DIL_BRANCHES[1:]:
            orders_shape.append(jax.ShapeDtypeStruct((B, rr, S // rr, B_WIDTH), BF16))
            orders_spec.append(pl.BlockSpec((None, rr, TM_IN // rr, B_WIDTH), phase))
    out_shape = (
        jax.ShapeDtypeStruct((B, A_WIDTH, S), BF16),
        jax.ShapeDtypeStruct((B, S, 2 * A_KV_WIDTH), BF16),
        jax.ShapeDtypeStruct((B, A_KV_WIDTH, S), BF16),
        *orders_shape,
    )
    return pl.pallas_call(
        _inproj_kernel,
        grid=(nt, B),
        in_specs=[
            pl.BlockSpec((None, TM_IN, D), tok),
            pl.BlockSpec((1, D), const),
            pl.BlockSpec((D, in_width), const),
            pl.BlockSpec((1, LANES), const),
            pl.BlockSpec((1, LANES), const),
            pl.BlockSpec((TM_IN, LANES), lambda t, b: (t, 0)),
            pl.BlockSpec((TM_IN, LANES), lambda t, b: (t, 0)),
            pl.BlockSpec((2 * LANES, LANES), const),
        ],
        out_specs=(
            pl.BlockSpec((None, A_WIDTH, TM_IN), lambda t, b: (b, 0, t)),
            pl.BlockSpec((None, TM_IN, 2 * A_KV_WIDTH), tok),
            pl.BlockSpec((None, A_KV_WIDTH, TM_IN), lambda t, b: (b, 0, t)),
            *orders_spec,
        ),
        out_shape=out_shape,
        scratch_shapes=[pltpu.VMEM((B_WIDTH // LANES, TM_IN, LANES), F32)] * 3
                       + [pltpu.VMEM((B_WIDTH // LANES, r_fine, TM_IN // r_fine, LANES), F32)] * 3,
        compiler_params=pltpu.CompilerParams(
            dimension_semantics=("arbitrary", "arbitrary"), vmem_limit_bytes=VMEM_LIMIT),
        name="in_projection",
    )(x, g, w_bf16, gq, gk, cos, sin, ones2)


def _attn_a_kernel(qt_ref, k_ref, vt0_ref, vt1_ref, g_ref, o_ref, s0_ref, s1_ref, m1_ref, half_ref):
    group = A_Q_HEADS // A_KV_HEADS
    width = group * TQ_A
    S = k_ref.shape[0]
    top = lax.broadcasted_iota(jnp.int32, (LANES, TQ_A), 0) < HEAD_DIM
    ones = jnp.ones((BF16_ROWS, KEY_CHUNK), BF16)

    @pl.when(pl.program_id(0) == 0)
    def _():
        s1_ref[...] = jnp.zeros_like(s1_ref)
        m1_ref[...] = jnp.zeros_like(m1_ref)
        half_ref[...] = jnp.zeros_like(half_ref)

    def half_step(kv, s_new_ref, s_old_ref, old_max, vt_old_ref):
        cols = []
        for j in range(group // 2):
            r0 = (kv * group + 2 * j) * HEAD_DIM
            qt2 = qt_ref[r0:r0 + LANES, :]
            zero = jnp.zeros_like(qt2)
            cols += [jnp.where(top, qt2, zero), jnp.where(top, zero, qt2)]
        w = jnp.concatenate(cols, axis=1)
        run_max = jnp.full((SUBLANES, width), -jnp.inf, F32)
        acc = jnp.zeros((HEAD_DIM + BF16_ROWS, width), F32)
        for c in range(S // KEY_CHUNK):
            rows = slice(c * KEY_CHUNK, (c + 1) * KEY_CHUNK)
            st = jnp.dot(k_ref[rows, kv * LANES:(kv + 1) * LANES], w,
                         preferred_element_type=F32)
            p = jnp.exp2(s_old_ref[rows, :] - old_max).astype(BF16)
            vaug = jnp.concatenate([vt_old_ref[:, rows], ones], axis=0)
            acc = acc + jnp.dot(vaug, p, preferred_element_type=F32)
            s_new_ref[rows, :] = st
            run_max = jnp.maximum(run_max, st.reshape(KEY_CHUNK // SUBLANES, SUBLANES, width).max(axis=0))
        ot = acc[0:HEAD_DIM] * (1.0 / acc[HEAD_DIM:HEAD_DIM + 1])
        return jnp.max(run_max, axis=0, keepdims=True), ot

    def head_blocks(ot):
        return [jnp.concatenate([ot[:, 2 * j * TQ_A:(2 * j + 1) * TQ_A],
                                 ot[:, (2 * j + 1) * TQ_A:(2 * j + 2) * TQ_A]], axis=0).T
                for j in range(group // 2)]

    m0, late = half_step(0, s0_ref, s1_ref, m1_ref[...], vt1_ref)

    y = jnp.concatenate(head_blocks(half_ref[...]) + head_blocks(late), axis=1)
    o_ref[...] = _rms(y, g_ref[...]).astype(BF16)
    m1, early = half_step(1, s1_ref, s0_ref, m0, vt0_ref)
    m1_ref[...] = m1
    half_ref[...] = early


def _attention_a(qat, ka, vat, g):
    B, _, S = qat.shape
    assert A_KV_HEADS == 2, "the step skew pairs exactly two kv groups"
    width = (A_Q_HEADS // A_KV_HEADS) * TQ_A
    nq = S // TQ_A
    n_steps = B * nq
    cur = lambda t: jnp.minimum(t, n_steps - 1)
    prev = lambda t: jnp.maximum(t - 1, 0)
    return pl.pallas_call(
        _attn_a_kernel,
        grid=(n_steps + 1,),
        in_specs=[
            pl.BlockSpec((None, A_WIDTH, TQ_A), lambda t: (cur(t) // nq, 0, cur(t) % nq)),
            pl.BlockSpec((None, S, 2 * A_KV_WIDTH), lambda t: (cur(t) // nq, 0, 0)),
            pl.BlockSpec((None, HEAD_DIM, S), lambda t: (cur(t) // nq, 0, 0)),
            pl.BlockSpec((None, HEAD_DIM, S), lambda t: (prev(t) // nq, 1, 0)),
            pl.BlockSpec((1, A_WIDTH), lambda t: (0, 0)),
        ],
        out_specs=pl.BlockSpec((None, TQ_A, A_WIDTH), lambda t: (prev(t) // nq, prev(t) % nq, 0)),
        out_shape=jax.ShapeDtypeStruct((B, S, A_WIDTH), BF16),
        scratch_shapes=[pltpu.VMEM((S, width), F32), pltpu.VMEM((S, width), F32),
                        pltpu.VMEM((1, width), F32), pltpu.VMEM((HEAD_DIM, width), F32)],
        compiler_params=pltpu.CompilerParams(
            dimension_semantics=("arbitrary",), vmem_limit_bytes=VMEM_LIMIT),
        name="attention_a",
    )(qat, ka, vat, vat, g)


def _t5_bucket(rel):
    rel = np.asarray(rel, np.int32)
    nb_half = N_BUCKETS // 2
    max_exact = nb_half // 2
    ret = np.where(rel > 0, nb_half, 0)
    n = np.abs(rel)
    large = max_exact + (np.log(np.maximum(n, 1).astype(np.float32) / max_exact)
                         / math.log(BUCKET_MAX_DIST / max_exact)
                         * (nb_half - max_exact)).astype(np.int32)
    large = np.minimum(large, nb_half - 1)
    return (ret + np.where(n < max_exact, n, large)).astype(np.int32)


def _variant_offset(v):
    return (0, -HALF_WINDOW, -2 * HALF_WINDOW)[v]


def _bias_kernel(rbt_ref, code_ref, valid_ref, o_ref):
    h = pl.program_id(0)
    rb = rbt_ref[pl.ds(h, 1), :]
    for n in range(len(DIL_BRANCHES)):
        code = code_ref[n:n + 1, :]
        acc = jnp.zeros(code.shape, F32)
        for b in range(N_BUCKETS):
            acc = jnp.where(code == b, rb[:, b:b + 1], acc)
        f = jnp.where(valid_ref[n:n + 1, :] > 0, acc * LOG2E, NEG)
        rows = jnp.broadcast_to(f, (TQ, BIAS_SPAN))
        for v in range(N_VARIANTS):
            shift = (KEY_WINDOW - _variant_offset(v)) % BIAS_SPAN
            o_ref[n * N_VARIANTS + v] = pltpu.roll(rows, shift, 1, stride=1, stride_axis=0)[:, 0:KEY_WINDOW]


def _bias_blocks(rel_bias):
    rel = np.arange(BIAS_SPAN, dtype=np.int32) - KEY_WINDOW
    code = np.stack([_t5_bucket(rel * r) for _, r in DIL_BRANCHES])
    valid = np.broadcast_to((np.abs(rel) <= HALF_WINDOW).astype(np.int32), code.shape)
    n_blocks = len(DIL_BRANCHES) * N_VARIANTS
    const = lambda h: (0, 0)
    return pl.pallas_call(
        _bias_kernel,
        grid=(B_HEADS,),
        in_specs=[pl.BlockSpec((B_HEADS, N_BUCKETS), const),
                  pl.BlockSpec(code.shape, const),
                  pl.BlockSpec(code.shape, const)],
        out_specs=pl.BlockSpec((n_blocks, None, TQ, KEY_WINDOW), lambda h: (0, h, 0, 0)),
        out_shape=jax.ShapeDtypeStruct((n_blocks, B_HEADS, TQ, KEY_WINDOW), F32),
        compiler_params=pltpu.CompilerParams(dimension_semantics=("arbitrary",)),
        name="bias_blocks",
    )(rel_bias.T, jnp.asarray(code), jnp.asarray(valid))


def _split_pair(q2):
    lane = lax.broadcasted_iota(jnp.int32, q2.shape, 1)
    low = lane < HEAD_DIM
    zero = jnp.zeros_like(q2)
    return jnp.concatenate([jnp.where(low, q2, zero), jnp.where(low, zero, q2)], axis=0)


def _attn_b_kernel(q1_ref, q4_ref, q16_ref, k1_ref, k4_ref, k16_ref, v1_ref, v4_ref, v16_ref,
                   bias_ref, g_ref, o_ref, y_ref, *acc_refs):
    pair = pl.program_id(1)
    S = q1_ref.shape[0]
    n_blocks = S // TQ
    low = lax.broadcasted_iota(jnp.int32, (TQ, LANES), 1) < HEAD_DIM
    q_refs = (q1_ref, q4_ref, q16_ref)
    k_refs = (k1_ref, k4_ref, k16_ref)
    v_refs = (v1_ref, v4_ref, v16_ref)

    def merge(x2):
        return jnp.where(low, x2[0:TQ], x2[TQ:2 * TQ])

    def banded_block(q2, k2, v2, bias):
        s = lax.dot_general(_split_pair(q2), k2, (((1,), (1,)), ((), ())),
                            preferred_element_type=F32) + bias
        m = jnp.max(s, axis=1, keepdims=True)
        p = jnp.exp2(s - m).astype(BF16)
        vaug = jnp.concatenate([v2, jnp.ones_like(v2)], axis=1)
        oa = jnp.dot(p, vaug, preferred_element_type=F32)
        return merge(oa[:, 0:LANES]), merge(jnp.broadcast_to(m, (2 * TQ, LANES))), merge(oa[:, LANES:2 * LANES])

    def branch_block(n, i):
        r = DIL_BRANCHES[n][1]
        length = S // r
        per_phase = length // TQ
        ph, blk = divmod(i, per_phase)
        if per_phase > 1:
            variant = 0 if blk == 0 else (2 if blk == per_phase - 1 else 1)
            w0 = min(max(blk * TQ - HALF_WINDOW, 0), length - KEY_WINDOW)
            width = KEY_WINDOW
        else:
            variant, w0, width = 0, 0, length
        r0 = blk * TQ
        tab = n * N_VARIANTS + variant
        bias = jnp.concatenate([bias_ref[tab, 2 * pair, :, 0:width],
                                bias_ref[tab, 2 * pair + 1, :, 0:width]], axis=0)
        lead = () if r == 1 else (ph,)
        q2 = q_refs[n][lead + (slice(r0, r0 + TQ), slice(None))]
        k2 = k_refs[n][lead + (slice(w0, w0 + width), slice(None))]
        v2 = v_refs[n][lead + (slice(w0, w0 + width), slice(None))]
        return banded_block(q2, k2, v2, bias), ph, r0

    for i in range(n_blocks):
        for n in range(1, len(DIL_BRANCHES)):
            r = DIL_BRANCHES[n][1]
            parts, ph, r0 = branch_block(n, i)
            rows = pl.ds(r0 * _row_pitch(r) + ph, TQ, stride=_row_pitch(r))
            for j, part in enumerate(parts):
                acc_refs[3 * (n - 1) + j][rows, :] = part

    def natural_rows(ref, r, r0):
        pitch = _row_pitch(r)
        if pitch == r:
            return ref[r0:r0 + TQ, :]
        runs = [ref[(r0 // r + q) * pitch:(r0 // r + q) * pitch + r, :] for q in range(TQ // r)]
        return jnp.concatenate(runs, axis=0)

    for i in range(n_blocks):
        parts, _, r0 = branch_block(0, i)
        branches = [parts] + [tuple(natural_rows(acc_refs[3 * (n - 1) + j], DIL_BRANCHES[n][1], r0) for j in range(3))
                              for n in range(1, len(DIL_BRANCHES))]
        top = functools.reduce(jnp.maximum, [m for _, m, _ in branches])
        scales = [jnp.exp2(m - top) for _, m, _ in branches]
        num = sum(w * o for w, (o, _, _) in zip(scales, branches))
        den = sum(w * d for w, (_, _, d) in zip(scales, branches))
        y_ref[pair, r0:r0 + TQ, :] = num * (1.0 / den)

    @pl.when(pair == B_HEADS // 2 - 1)
    def _():
        y = jnp.concatenate([y_ref[c] for c in range(B_HEADS // 2)], axis=1)
        o_ref[...] = _rms(y, g_ref[...]).astype(BF16)


def _row_pitch(r):
    return r + 1 if r % SUBLANES == 0 else r


def _attention_b(orders, bias, g):
    B, S, _ = orders[0].shape
    specs = []
    for a in orders:
        if a.ndim == 3:
            specs.append(pl.BlockSpec((None, S, LANES), lambda b, j: (b, 0, j)))
        else:
            specs.append(pl.BlockSpec((None, a.shape[1], a.shape[2], LANES), lambda b, j: (b, 0, 0, j)))
    return pl.pallas_call(
        _attn_b_kernel,
        grid=(B, B_HEADS // 2),
        in_specs=specs + [
            pl.BlockSpec(bias.shape, lambda b, j: (0, 0, 0, 0), pipeline_mode=pl.Buffered(1)),
            pl.BlockSpec((1, B_WIDTH), lambda b, j: (0, 0)),
        ],
        out_specs=pl.BlockSpec((None, S, B_WIDTH), lambda b, j: (b, 0, 0)),
        out_shape=jax.ShapeDtypeStruct((B, S, B_WIDTH), BF16),
        scratch_shapes=[pltpu.VMEM((B_HEADS // 2, S, LANES), F32)]
                       + [pltpu.VMEM((S // r * _row_pitch(r), LANES), F32) for _, r in DIL_BRANCHES[1:] for _ in range(3)],
        compiler_params=pltpu.CompilerParams(
            dimension_semantics=("arbitrary", "arbitrary"), vmem_limit_bytes=VMEM_LIMIT),
        name="attention_b",
    )(*orders, bias, g)


def _out_mlp_kernel(x_ref, ma_ref, mb_ref, wo_ref, gm_ref, wup_ref, wdn_ref, gf_ref, o_ref):
    mix = jnp.concatenate([ma_ref[...], mb_ref[...]], axis=1)
    h = x_ref[...] + jnp.dot(mix, wo_ref[...], preferred_element_type=F32)
    hn = _rms(h, gm_ref[...]).astype(BF16)
    acc = h
    d_ff = wup_ref.shape[1]
    for c in range(d_ff // FF_CHUNK):
        sl = slice(c * FF_CHUNK, (c + 1) * FF_CHUNK)
        u = jnp.maximum(jnp.dot(hn, wup_ref[:, sl], preferred_element_type=F32), 0.0)
        acc = acc + jnp.dot((u * u).astype(BF16), wdn_ref[sl, :], preferred_element_type=F32)
    o_ref[...] = _rms(acc, gf_ref[...])


def _out_mlp(x2, ma, mb, wo, gm, wup, wdn, gf):
    N, D = x2.shape
    d_ff = wup.shape[1]
    tok = lambda t: (t, 0)
    const = lambda t: (0, 0)
    resident = functools.partial(pl.BlockSpec, index_map=const, pipeline_mode=pl.Buffered(1))
    return pl.pallas_call(
        _out_mlp_kernel,
        grid=(N // TM_OUT,),
        in_specs=[
            pl.BlockSpec((TM_OUT, D), tok),
            pl.BlockSpec((TM_OUT, A_WIDTH), tok),
            pl.BlockSpec((TM_OUT, B_WIDTH), tok),
            resident((A_WIDTH + B_WIDTH, D)),
            pl.BlockSpec((1, D), const),
            resident((D, d_ff)),
            resident((d_ff, D)),
            pl.BlockSpec((1, D), const),
        ],
        out_specs=pl.BlockSpec((TM_OUT, D), tok),
        out_shape=jax.ShapeDtypeStruct((N, D), F32),
        compiler_params=pltpu.CompilerParams(
            dimension_semantics=("arbitrary",), vmem_limit_bytes=VMEM_LIMIT),
        name="out_mlp",
    )(x2, ma, mb, wo, gm, wup, wdn, gf)


def kernel(x, attn_norm_g, w_in, q_norm_g, k_norm_g, rel_bias, out_norm_a_g, out_norm_b_g,
           w_out, mlp_norm_g, w_up, w_down, final_norm_g):
    B, S, D = x.shape
    assert w_in.shape[0] == 1, "the output kernel fuses the final norm: single layer only"
    bias = _bias_blocks(rel_bias)
    gq = jnp.tile(q_norm_g[0], LANES // HEAD_DIM).reshape(1, LANES)
    gk = jnp.tile(k_norm_g[0], LANES // HEAD_DIM).reshape(1, LANES)
    qat, ka, vat, *orders = _in_projection(
        x, attn_norm_g[0].reshape(1, D), w_in[0].astype(BF16), gq, gk)
    ma = _attention_a(qat, ka, vat, out_norm_a_g[0].reshape(1, A_WIDTH))
    mb = _attention_b(orders, bias, out_norm_b_g[0].reshape(1, B_WIDTH))
    out = _out_mlp(x.reshape(B * S, D), ma.reshape(B * S, A_WIDTH), mb.reshape(B * S, B_WIDTH),
                   w_out[0].astype(BF16), mlp_norm_g[0].reshape(1, D),
                   w_up[0].astype(BF16), w_down[0].astype(BF16), final_norm_g.reshape(1, D))
    return out.reshape(B, S, D)
```
